```python
import math
import jax, jax.numpy as jnp
from jax import lax

D_MODEL = 1024
BATCH = 4
SEQ = 4096
DEPTH = 2

RET_HEADS = 4
RET_DK = 128
RET_DV = 128
RET_CHUNK = 128
ROPE_BASE = 10000.0
S5_WIDTH = D_MODEL // 2
S5_GROUP = 16
S5_GROUPS = S5_WIDTH // S5_GROUP
S5_STATE = 64
S5_DT_MIN = 1e-3
S5_DT_MAX = 1e-1
ATT_HEADS = 8
ATT_DH = D_MODEL // ATT_HEADS
DILATED_BRANCHES = ((128, 1), (512, 4), (2048, 16))
N_GROUPS = 4
EXPERTS_PER_GROUP = 4
TOP_K = 2
D_EXPERT = 512
RMS_EPS = 1e-6
GN_EPS = 1e-6
N_EVEN = (DEPTH + 1) // 2
N_ODD = DEPTH // 2
RET_QK_W = RET_HEADS * RET_DK
RET_V_W = RET_HEADS * RET_DV
AB_IN_W = 2 * RET_QK_W + 2 * RET_V_W + S5_WIDTH
AB_OUT_W = RET_V_W + S5_WIDTH

kernel_name = "hybrid_retention_s5_dilated_hmoe"


def rmsnorm(x, g):
    xf = x.astype(jnp.float32)
    y = xf * lax.rsqrt(jnp.mean(xf * xf, axis=-1, keepdims=True) + RMS_EPS)
    return (y * g.astype(jnp.float32)).astype(x.dtype)


def rotary(x, pos):
    half = x.shape[-1] // 2
    inv = ROPE_BASE ** (-jnp.arange(half, dtype=jnp.float32) / half)
    ang = pos.astype(jnp.float32)[:, None] * inv[None, :]
    cos = jnp.cos(ang)[None, :, None, :]
    sin = jnp.sin(ang)[None, :, None, :]
    xf = x.astype(jnp.float32)
    x1, x2 = xf[..., :half], xf[..., half:]
    return jnp.concatenate([x1 * cos - x2 * sin, x1 * sin + x2 * cos], axis=-1)


def retention(q, k, v):
    B, L, H, _ = q.shape
    C = RET_CHUNK
    N = L // C
    lg = jnp.log1p(-jnp.exp2(-5.0 - jnp.arange(H, dtype=jnp.float32)))
    idx = jnp.arange(C, dtype=jnp.float32)
    diff = idx[:, None] - idx[None, :]
    decay = jnp.where(diff[None] >= 0, jnp.exp(jnp.maximum(diff, 0.0)[None] * lg[:, None, None]), 0.0)
    xi = jnp.exp((idx + 1.0)[None, :] * lg[:, None])
    zeta = jnp.exp((C - 1.0 - idx)[None, :] * lg[:, None])
    chunk_decay = jnp.exp(C * lg)

    def to_chunks(t):
        return t.reshape(B, N, C, H, t.shape[-1]).transpose(0, 3, 1, 2, 4)

    qc, kc, vc = to_chunks(q), to_chunks(k), to_chunks(v)
    scores = jnp.einsum('bhncd,bhnsd->bhncs', qc, kc) * decay[None, :, None]
    intra = jnp.einsum('bhncs,bhnse->bhnce', scores, vc)
    kv = jnp.einsum('bhnsd,bhnse->bhnde', kc * zeta[None, :, None, :, None], vc)

    def step(state, kv_n):
        return state * chunk_decay[None, :, None, None] + kv_n, state

    init = jnp.zeros((B, H, kv.shape[-2], kv.shape[-1]), jnp.float32)
    _, prev = lax.scan(step, init, jnp.moveaxis(kv, 2, 0))
    prev = jnp.moveaxis(prev, 0, 2)
    cross = jnp.einsum('bhncd,bhnde->bhnce', qc * xi[None, :, None, :, None], prev)
    return (intra + cross).transpose(0, 2, 3, 1, 4).reshape(B, L, H, -1)


def s5_mixer(u, a_re, a_im, b_re, b_im, c_re, c_im, d_skip, log_dt, w_glu, b_glu):
    B, L, _ = u.shape
    f32 = jnp.float32
    uf = u.astype(f32).reshape(B, L, S5_GROUPS, S5_GROUP)
    a_re, a_im = a_re.astype(f32), a_im.astype(f32)
    b_re, b_im = b_re.astype(f32), b_im.astype(f32)
    dt = jnp.exp(log_dt.astype(f32))[:, None]
    zr, zi = a_re * dt, a_im * dt
    mag = jnp.exp(zr)
    lam_re, lam_im = mag * jnp.cos(zi), mag * jnp.sin(zi)
    nr, ni = lam_re - 1.0, lam_im
    den = a_re * a_re + a_im * a_im
    coef_re = (nr * a_re + ni * a_im) / den
    coef_im = (ni * a_re - nr * a_im) / den
    bb_re = coef_re[..., None] * b_re - coef_im[..., None] * b_im
    bb_im = coef_re[..., None] * b_im + coef_im[..., None] * b_re
    bu_re = jnp.einsum('blgh,gph->blgp', uf, bb_re)
    bu_im = jnp.einsum('blgh,gph->blgp', uf, bb_im)
    ar = jnp.broadcast_to(lam_re, bu_re.shape)
    ai = jnp.broadcast_to(lam_im, bu_im.shape)

    def combine(e1, e2):
        a1r, a1i, b1r, b1i = e1
        a2r, a2i, b2r, b2i = e2
        return (a1r * a2r - a1i * a2i,
                a1r * a2i + a1i * a2r,
                a2r * b1r - a2i * b1i + b2r,
                a2r * b1i + a2i * b1r + b2i)

    _, _, xr, xim = lax.associative_scan(combine, (ar, ai, bu_re, bu_im), axis=1)
    y = (jnp.einsum('blgp,ghp->blgh', xr, c_re.astype(f32))
         - jnp.einsum('blgp,ghp->blgh', xim, c_im.astype(f32))
         + d_skip.astype(f32) * uf)
    g = jax.nn.gelu(y.reshape(B, L, S5_WIDTH))
    return g * jax.nn.sigmoid(g @ w_glu.astype(f32) + b_glu.astype(f32))


def parallel_retention_s5(xn, w_in, w_out, a_re, a_im, b_re, b_im, c_re, c_im, d_skip, log_dt, w_glu, b_glu, pos):
    B, L, _ = xn.shape
    proj = xn @ w_in
    q, k, v, gate, u = jnp.split(
        proj, [RET_QK_W, 2 * RET_QK_W, 2 * RET_QK_W + RET_V_W, 2 * RET_QK_W + 2 * RET_V_W], axis=-1)
    q = rotary(q.reshape(B, L, RET_HEADS, RET_DK), pos)
    k = rotary(k.reshape(B, L, RET_HEADS, RET_DK), pos) * (RET_DK ** -0.5)
    v = v.astype(jnp.float32).reshape(B, L, RET_HEADS, RET_DV)
    o = retention(q, k, v)
    mu = jnp.mean(o, axis=-1, keepdims=True)
    var = jnp.mean(jnp.square(o - mu), axis=-1, keepdims=True)
    o = (o - mu) * lax.rsqrt(var + GN_EPS)
    ret_out = jax.nn.silu(gate.astype(jnp.float32)) * o.reshape(B, L, RET_V_W)
    s5_out = s5_mixer(u, a_re, a_im, b_re, b_im, c_re, c_im, d_skip, log_dt, w_glu, b_glu)
    merged = jnp.concatenate([ret_out, s5_out], axis=-1).astype(xn.dtype)
    return merged @ w_out


def dilated_branch(q, k, v, window, dilation):
    B, L, H, dh = q.shape
    band = window // dilation
    ls = L // dilation
    nb = -(-ls // band)
    pad = nb * band - ls

    def to_sub(t):
        t = t.reshape(B, ls, dilation, H, dh).transpose(0, 2, 3, 1, 4)
        t = jnp.pad(t, ((0, 0), (0, 0), (0, 0), (0, pad), (0, 0)))
        return t.reshape(B, dilation, H, nb, band, dh)

    def with_prev(t):
        prev = jnp.pad(t, ((0, 0), (0, 0), (0, 0), (1, 0), (0, 0), (0, 0)))[:, :, :, :-1]
        return jnp.concatenate([prev, t], axis=4)

    qs = to_sub(q)
    kb, vb = with_prev(to_sub(k)), with_prev(to_sub(v))
    s = jnp.einsum('brhnqd,brhnkd->brhnqk', qs, kb) * (dh ** -0.5)
    qi = jnp.arange(band)[:, None]
    kj = jnp.arange(2 * band)[None, :]
    dist = band + qi - kj
    in_band = (dist >= 0) & (dist <= band)
    has_prev = (jnp.arange(nb)[:, None, None] > 0) | (kj >= band)[None]
    valid = in_band[None] & has_prev
    s = jnp.where(valid, s, -jnp.inf)
    m = jnp.max(s, axis=-1, keepdims=True)
    p = jnp.exp(s - m)
    l = jnp.sum(p, axis=-1, keepdims=True)
    o = jnp.einsum('brhnqk,brhnkd->brhnqd', p, vb) / l
    lse = (m + jnp.log(l))[..., 0]
    o = o.reshape(B, dilation, H, nb * band, dh)[:, :, :, :ls].transpose(0, 3, 1, 2, 4).reshape(B, L, H, dh)
    lse = lse.reshape(B, dilation, H, nb * band)[..., :ls].transpose(0, 3, 1, 2).reshape(B, L, H)
    return o, lse


def dilated_attention(xn, w_qkv, w_out):
    B, L, _ = xn.shape
    qkv = (xn @ w_qkv).astype(jnp.float32).reshape(B, L, 3, ATT_HEADS, ATT_DH)
    q, k, v = qkv[:, :, 0], qkv[:, :, 1], qkv[:, :, 2]
    outs, lses = [], []
    for window, dilation in DILATED_BRANCHES:
        o, lse = dilated_branch(q, k, v, window, dilation)
        outs.append(o)
        lses.append(lse)
    wts = jax.nn.softmax(jnp.stack(lses, axis=0), axis=0)
    o = jnp.sum(wts[..., None] * jnp.stack(outs, axis=0), axis=0)
    return o.reshape(B, L, ATT_HEADS * ATT_DH).astype(xn.dtype) @ w_out


def hier_moe(xn, w_group, b_group, w_router, b_router, w_gate, w_up, w_down):
    B, L, D = xn.shape
    t = xn.reshape(B * L, D)
    tf = t.astype(jnp.float32)
    glog = tf @ w_group.astype(jnp.float32) + b_group.astype(jnp.float32)
    gval, gsel = lax.top_k(jax.nn.softmax(glog, axis=-1), 1)
    elog = jnp.einsum('td,dge->tge', tf, w_router.astype(jnp.float32)) + b_router.astype(jnp.float32)
    elog_sel = jnp.take_along_axis(elog, gsel[:, :, None], axis=1)[:, 0]
    top_v, top_i = lax.top_k(elog_sel, TOP_K)
    top_w = jax.nn.softmax(top_v, axis=-1) * gval
    w_in_group = jnp.sum(jax.nn.one_hot(top_i, EXPERTS_PER_GROUP) * top_w[..., None], axis=1)
    combine = jax.nn.one_hot(gsel[:, 0], N_GROUPS)[:, :, None] * w_in_group[:, None, :]
    out = jnp.zeros((B * L, D), jnp.float32)
    for g in range(N_GROUPS):
        h = jax.nn.silu(jnp.einsum('td,edf->tef', t, w_gate[g])) * jnp.einsum('td,edf->tef', t, w_up[g])
        out = out + jnp.einsum('tef,efd->td', h * combine[:, g, :, None], w_down[g])
    return out.reshape(B, L, D).astype(xn.dtype)


def _normal(k, shape, scale):
    return jax.random.normal(k, shape, jnp.float32) * scale


def setup_inputs(seed: int = 0) -> dict:
    key = jax.random.key(seed)
    ks = jax.random.split(key, 32)
    G, P, Hs = S5_GROUPS, S5_STATE, S5_GROUP
    a_im0 = math.pi * jnp.arange(P, dtype=jnp.float32)
    return {
        "x": _normal(ks[0], (BATCH, SEQ, D_MODEL), 1.0),
        "mix_norm": 1.0 + _normal(ks[1], (DEPTH, D_MODEL), 0.01),
        "ffn_norm": 1.0 + _normal(ks[2], (DEPTH, D_MODEL), 0.01),
        "final_norm": 1.0 + _normal(ks[3], (D_MODEL,), 0.01),
        "ab_w_in": _normal(ks[4], (N_EVEN, D_MODEL, AB_IN_W), D_MODEL ** -0.5),
        "ab_w_out": _normal(ks[5], (N_EVEN, AB_OUT_W, D_MODEL), AB_OUT_W ** -0.5),
        "s5_a_re": -0.5 + _normal(ks[6], (N_EVEN, G, P), 0.01),
        "s5_a_im": a_im0[None, None, :] + _normal(ks[7], (N_EVEN, G, P), 0.01),
        "s5_b_re": _normal(ks[8], (N_EVEN, G, P, Hs), Hs ** -0.5),
        "s5_b_im": _normal(ks[9], (N_EVEN, G, P, Hs), Hs ** -0.5),
        "s5_c_re": _normal(ks[10], (N_EVEN, G, Hs, P), P ** -0.5),
        "s5_c_im": _normal(ks[11], (N_EVEN, G, Hs, P), P ** -0.5),
        "s5_d": _normal(ks[12], (N_EVEN, G, Hs), 1.0),
        "s5_log_dt": jax.random.uniform(ks[13], (N_EVEN, G), jnp.float32,
                                        math.log(S5_DT_MIN), math.log(S5_DT_MAX)),
        "s5_w_glu": _normal(ks[14], (N_EVEN, S5_WIDTH, S5_WIDTH), S5_WIDTH ** -0.5),
        "s5_b_glu": _normal(ks[15], (N_EVEN, S5_WIDTH), 0.01),
        "c_w_qkv": _normal(ks[16], (N_ODD, D_MODEL, 3 * ATT_HEADS * ATT_DH), D_MODEL ** -0.5),
        "c_w_out": _normal(ks[17], (N_ODD, ATT_HEADS * ATT_DH, D_MODEL), (ATT_HEADS * ATT_DH) ** -0.5),
        "moe_w_group": _normal(ks[18], (DEPTH, D_MODEL, N_GROUPS), D_MODEL ** -0.5),
        "moe_b_group": _normal(ks[19], (DEPTH, N_GROUPS), 0.01),
        "moe_w_router": _normal(ks[20], (DEPTH, D_MODEL, N_GROUPS, EXPERTS_PER_GROUP), D_MODEL ** -0.5),
        "moe_b_router": _normal(ks[21], (DEPTH, N_GROUPS, EXPERTS_PER_GROUP), 0.01),
        "moe_w_gate": _normal(ks[22], (DEPTH, N_GROUPS, EXPERTS_PER_GROUP, D_MODEL, D_EXPERT), D_MODEL ** -0.5),
        "moe_w_up": _normal(ks[23], (DEPTH, N_GROUPS, EXPERTS_PER_GROUP, D_MODEL, D_EXPERT), D_MODEL ** -0.5),
        "moe_w_down": _normal(ks[24], (DEPTH, N_GROUPS, EXPERTS_PER_GROUP, D_EXPERT, D_MODEL), D_EXPERT ** -0.5),
    }


def reference(x, mix_norm, ffn_norm, final_norm, ab_w_in, ab_w_out, s5_a_re, s5_a_im, s5_b_re, s5_b_im,
              s5_c_re, s5_c_im, s5_d, s5_log_dt, s5_w_glu, s5_b_glu, c_w_qkv, c_w_out,
              moe_w_group, moe_b_group, moe_w_router, moe_b_router, moe_w_gate, moe_w_up, moe_w_down):
    h = x
    pos = jnp.arange(x.shape[1])
    for layer in range(DEPTH):
        i = layer // 2
        hn = rmsnorm(h, mix_norm[layer])
        if layer % 2 == 0:
            mixed = parallel_retention_s5(hn, ab_w_in[i], ab_w_out[i], s5_a_re[i], s5_a_im[i], s5_b_re[i],
                                          s5_b_im[i], s5_c_re[i], s5_c_im[i], s5_d[i], s5_log_dt[i],
                                          s5_w_glu[i], s5_b_glu[i], pos)
        else:
            mixed = dilated_attention(hn, c_w_qkv[i], c_w_out[i])
        h = h + mixed
        h = h + hier_moe(rmsnorm(h, ffn_norm[layer]), moe_w_group[layer], moe_b_group[layer],
                         moe_w_router[layer], moe_b_router[layer], moe_w_gate[layer],
                         moe_w_up[layer], moe_w_down[layer])
    return rmsnorm(h, final_norm)
```

```python
import functools
import math

import jax
import jax.numpy as jnp
from jax import lax
from jax.experimental import pallas as pl
from jax.experimental.pallas import tpu as pltpu

F32 = jnp.float32
BF16 = jnp.bfloat16

RET_HEADS = 4
RET_DK = 128
ROPE_BASE = 10000.0
S5_GROUP = 16
S5_STATE = 64
ATT_HEADS = 8
ATT_DH = 128
DILATED_BRANCHES = ((128, 1), (512, 4), (2048, 16))
N_GROUPS = 4
EXPERTS_PER_GROUP = 4
N_EXPERTS = N_GROUPS * EXPERTS_PER_GROUP
RMS_EPS = 1e-6
GN_EPS = 1e-6

LANES = 128
VMEM_LIMIT = 48 * 1024 * 1024
S5_CHUNK = 16
RET_CHUNK = 256
NEG_BIG = -1e30


def _cparams(*sem):
    return pltpu.CompilerParams(dimension_semantics=sem, vmem_limit_bytes=VMEM_LIMIT)


def _rms(x, g):
    return x * lax.rsqrt(jnp.mean(x * x, axis=-1, keepdims=True) + RMS_EPS) * g


def _norm_matmul_kernel(x_ref, g_ref, w_ref, o_ref):
    y = _rms(x_ref[...], g_ref[...])
    o_ref[...] = jnp.dot(y.astype(BF16), w_ref[...], preferred_element_type=F32)


def norm_matmul(x, g, w_bf16, tm=256):
    T, D = x.shape
    N = w_bf16.shape[1]
    return pl.pallas_call(
        _norm_matmul_kernel,
        grid=(T // tm,),
        in_specs=[pl.BlockSpec((tm, D), lambda i: (i, 0)),
                  pl.BlockSpec((1, D), lambda i: (0, 0)),
                  pl.BlockSpec((D, N), lambda i: (0, 0))],
        out_specs=pl.BlockSpec((tm, N), lambda i: (i, 0)),
        out_shape=jax.ShapeDtypeStruct((T, N), F32),
        compiler_params=_cparams("parallel"),
        name="norm_matmul",
    )(x, g.reshape(1, D), w_bf16)


def _matmul_res_kernel(a_ref, w_ref, r_ref, o_ref):
    o_ref[...] = r_ref[...] + jnp.dot(a_ref[...].astype(BF16), w_ref[...], preferred_element_type=F32)


def matmul_residual(a, w_bf16, res, tm=256):
    T, K = a.shape
    N = w_bf16.shape[1]
    return pl.pallas_call(
        _matmul_res_kernel,
        grid=(T // tm,),
        in_specs=[pl.BlockSpec((tm, K), lambda i: (i, 0)),
                  pl.BlockSpec((K, N), lambda i: (0, 0)),
                  pl.BlockSpec((tm, N), lambda i: (i, 0))],
        out_specs=pl.BlockSpec((tm, N), lambda i: (i, 0)),
        out_shape=jax.ShapeDtypeStruct((T, N), F32),
        compiler_params=_cparams("parallel"),
        name="matmul_residual",
    )(a, w_bf16, res)


def _retention_kernel(q_ref, k_ref, v_ref, gate_ref, cos_ref, sin_ref, decay_ref, xi_ref, zeta_ref, cd_ref,
                      o_ref, state_ref, *, n_chunks, chunk):
    state_ref[...] = jnp.zeros_like(state_ref)
    decay = decay_ref[...]
    xi = xi_ref[...]
    zeta = zeta_ref[...]
    cd = cd_ref[...]
    half = RET_DK // 2

    def rot(x, cos, sin):
        return x * cos + pltpu.roll(x, half, axis=1) * sin

    def body(n, carry):
        rows = pl.ds(pl.multiple_of(n * chunk, chunk), chunk)
        cos = cos_ref[rows, :]
        sin = sin_ref[rows, :]
        q = rot(q_ref[rows, :], cos, sin)
        k = rot(k_ref[rows, :], cos, sin) * (RET_DK ** -0.5)
        v = v_ref[rows, :].astype(BF16)
        s = lax.dot_general(q.astype(BF16), k.astype(BF16), (((1,), (1,)), ((), ())),
                            preferred_element_type=F32) * decay
        state = state_ref[...]
        o = jnp.dot(s.astype(BF16), v, preferred_element_type=F32)
        o = o + jnp.dot((q * xi).astype(BF16), state.astype(BF16), preferred_element_type=F32)
        kv = lax.dot_general((k * zeta).astype(BF16), v, (((0,), (0,)), ((), ())),
                             preferred_element_type=F32)
        state_ref[...] = state * cd + kv
        mu = jnp.mean(o, axis=-1, keepdims=True)
        oc = o - mu
        var = jnp.mean(oc * oc, axis=-1, keepdims=True)
        on = oc * lax.rsqrt(var + GN_EPS)
        g = gate_ref[rows, :]
        o_ref[rows, :] = g * jax.nn.sigmoid(g) * on
        return carry

    lax.fori_loop(0, n_chunks, body, 0)


def retention_mixer(proj, B, L, chunk=RET_CHUNK):
    H, dk = RET_HEADS, RET_DK
    chunk = min(chunk, L)
    half = dk // 2
    pos = jnp.arange(L, dtype=F32)
    inv = ROPE_BASE ** (-jnp.arange(half, dtype=F32) / half)
    ang = pos[:, None] * inv[None, :]
    cos = jnp.concatenate([jnp.cos(ang), jnp.cos(ang)], axis=-1)
    sin = jnp.concatenate([-jnp.sin(ang), jnp.sin(ang)], axis=-1)
    lg = jnp.log1p(-jnp.exp2(-5.0 - jnp.arange(H, dtype=F32)))
    idx = jnp.arange(chunk, dtype=F32)
    diff = idx[:, None] - idx[None, :]
    decay = jnp.where(diff[None] >= 0, jnp.exp(jnp.maximum(diff, 0.0)[None] * lg[:, None, None]), 0.0)
    xi = jnp.exp((idx + 1.0)[None, :] * lg[:, None])[..., None]
    zeta = jnp.exp((chunk - 1.0 - idx)[None, :] * lg[:, None])[..., None]
    cd = jnp.exp(chunk * lg)[:, None, None]

    def col(off):
        return pl.BlockSpec((L, dk), lambda b, h: (b, off + h))

    tab = pl.BlockSpec((L, dk), lambda b, h: (0, 0))
    return pl.pallas_call(
        functools.partial(_retention_kernel, n_chunks=L // chunk, chunk=chunk),
        grid=(B, H),
        in_specs=[col(0), col(H), col(2 * H), col(3 * H), tab, tab,
                  pl.BlockSpec((None, chunk, chunk), lambda b, h: (h, 0, 0)),
                  pl.BlockSpec((None, chunk, 1), lambda b, h: (h, 0, 0)),
                  pl.BlockSpec((None, chunk, 1), lambda b, h: (h, 0, 0)),
                  pl.BlockSpec((None, 1, 1), lambda b, h: (h, 0, 0))],
        out_specs=pl.BlockSpec((L, dk), lambda b, h: (b, h)),
        out_shape=jax.ShapeDtypeStruct((B * L, H * dk), F32),
        scratch_shapes=[pltpu.VMEM((dk, dk), F32)],
        compiler_params=_cparams("parallel", "parallel"),
        name="retention",
    )(proj, proj, proj, proj, cos, sin, decay, xi, zeta, cd)


def _s5_kernel(u_ref, kt_ref, bre_ref, bim_ref, cre_ref, cim_ref, lre_ref, lim_ref, y_ref,
               vre_ref, vim_ref, sre_ref, sim_ref, *, groups, n_chunks, batch):
    for j in range(groups):
        u = u_ref[j]
        vre_ref[j] = jnp.dot(u, bre_ref[j], preferred_element_type=F32)
        vim_ref[j] = jnp.dot(u, bim_ref[j], preferred_element_type=F32)

    lre = [lre_ref[j] for j in range(groups)]
    lim = [lim_ref[j] for j in range(groups)]

    def step(n, carry):
        rows = pl.ds(n * batch, batch)
        new = []
        for j in range(groups):
            sr, si = carry[2 * j], carry[2 * j + 1]
            sre_ref[j, rows, :] = sr
            sim_ref[j, rows, :] = si
            new.append(sr * lre[j] - si * lim[j] + vre_ref[j, rows, :])
            new.append(sr * lim[j] + si * lre[j] + vim_ref[j, rows, :])
        return tuple(new)

    zero = jnp.zeros((batch, S5_STATE), F32)
    lax.fori_loop(0, n_chunks, step, tuple(zero for _ in range(2 * groups)))

    for j in range(groups):
        y = jnp.dot(u_ref[j], kt_ref[j], preferred_element_type=F32)
        y = y + jnp.dot(sre_ref[j].astype(BF16), cre_ref[j], preferred_element_type=F32)
        y = y + jnp.dot(sim_ref[j].astype(BF16), cim_ref[j], preferred_element_type=F32)
        y_ref[j] = y


def _s5_tables(a_re, a_im, b_re, b_im, c_re, c_im, log_dt):
    C = S5_CHUNK
    G, P = a_re.shape
    Hs = S5_GROUP
    dt = jnp.exp(log_dt)[:, None]
    zr, zi = a_re * dt, a_im * dt
    mag = jnp.exp(zr)
    lam_re, lam_im = mag * jnp.cos(zi), mag * jnp.sin(zi)
    nr, ni = lam_re - 1.0, lam_im
    den = a_re * a_re + a_im * a_im
    coef_re = (nr * a_re + ni * a_im) / den
    coef_im = (ni * a_re - nr * a_im) / den
    bb_re = coef_re[..., None] * b_re - coef_im[..., None] * b_im
    bb_im = coef_re[..., None] * b_im + coef_im[..., None] * b_re
    tau = jnp.arange(C + 1, dtype=F32)[:, None, None]
    pmag = jnp.exp(tau * zr[None])
    pw_re, pw_im = pmag * jnp.cos(tau * zi[None]), pmag * jnp.sin(tau * zi[None])
    lb_re = pw_re[..., None] * bb_re[None] - pw_im[..., None] * bb_im[None]
    lb_im = pw_re[..., None] * bb_im[None] + pw_im[..., None] * bb_re[None]
    kern = (jnp.einsum('ghp,tgpk->tghk', c_re, lb_re[:C]) - jnp.einsum('ghp,tgpk->tghk', c_im, lb_im[:C]))
    s = jnp.arange(C)
    lag = s[None, :] - s[:, None]
    kt = jnp.where((lag >= 0)[:, :, None, None, None], kern[jnp.clip(lag, 0, C - 1)], 0.0)
    kt = kt.transpose(2, 0, 4, 1, 3).reshape(G, C * Hs, C * Hs)
    bst_re = lb_re[:C][::-1].transpose(1, 0, 3, 2).reshape(G, C * Hs, P)
    bst_im = lb_im[:C][::-1].transpose(1, 0, 3, 2).reshape(G, C * Hs, P)
    cl_re = c_re[None] * pw_re[1:, :, None, :] - c_im[None] * pw_im[1:, :, None, :]
    cl_im = c_re[None] * pw_im[1:, :, None, :] + c_im[None] * pw_re[1:, :, None, :]
    cst_re = cl_re.transpose(1, 3, 0, 2).reshape(G, P, C * Hs)
    cst_im = (-cl_im).transpose(1, 3, 0, 2).reshape(G, P, C * Hs)
    return (kt.astype(BF16), bst_re.astype(BF16), bst_im.astype(BF16), cst_re.astype(BF16), cst_im.astype(BF16),
            pw_re[C][:, None, :], pw_im[C][:, None, :])


def s5_mixer(u, B, L, a_re, a_im, b_re, b_im, c_re, c_im, log_dt, groups_per_step=4):
    C, Hs, P = S5_CHUNK, S5_GROUP, S5_STATE
    G = a_re.shape[0]
    Nn = L // C
    R = Nn * B
    kt, bst_re, bst_im, cst_re, cst_im, l_re, l_im = _s5_tables(a_re, a_im, b_re, b_im, c_re, c_im, log_dt)
    ug = u.astype(BF16).reshape(B, Nn, C, G, Hs).transpose(3, 1, 0, 2, 4).reshape(G, R, C * Hs)
    gs = groups_per_step
    W = C * Hs

    def gspec(*shape):
        return pl.BlockSpec((gs,) + shape, lambda i: (i,) + (0,) * len(shape))

    y = pl.pallas_call(
        functools.partial(_s5_kernel, groups=gs, n_chunks=Nn, batch=B),
        grid=(G // gs,),
        in_specs=[gspec(R, W), gspec(W, W), gspec(W, P), gspec(W, P), gspec(P, W), gspec(P, W),
                  gspec(1, P), gspec(1, P)],
        out_specs=gspec(R, W),
        out_shape=jax.ShapeDtypeStruct((G, R, W), F32),
        scratch_shapes=[pltpu.VMEM((gs, R, P), F32) for _ in range(4)],
        compiler_params=_cparams("parallel"),
        name="s5_chunked",
    )(ug, kt, bst_re, bst_im, cst_re, cst_im, l_re, l_im)
    return y.reshape(G, Nn, B, C, Hs).transpose(2, 1, 3, 0, 4).reshape(B * L, G * Hs)


def _ab_out_kernel(ret_ref, y_ref, u_ref, d_ref, wglu_ref, bglu_ref, wo1_ref, wo2_ref, h_ref, o_ref):
    y = y_ref[...] + d_ref[...] * u_ref[...]
    g = jax.nn.gelu(y)
    z = jnp.dot(g.astype(BF16), wglu_ref[...], preferred_element_type=F32) + bglu_ref[...]
    s5 = g * jax.nn.sigmoid(z)
    acc = jnp.dot(ret_ref[...].astype(BF16), wo1_ref[...], preferred_element_type=F32)
    acc = acc + jnp.dot(s5.astype(BF16), wo2_ref[...], preferred_element_type=F32)
    o_ref[...] = h_ref[...] + acc


def ab_out(ret, y, proj, d_skip, w_glu, b_glu, w_out, h, tm=256):
    T, Wv = ret.shape
    Ws = y.shape[1]
    D = h.shape[1]
    u_col = (proj.shape[1] - Ws) // Ws
    row = lambda w: pl.BlockSpec((tm, w), lambda i: (i, 0))
    full = lambda a, b: pl.BlockSpec((a, b), lambda i: (0, 0))
    return pl.pallas_call(
        _ab_out_kernel,
        grid=(T // tm,),
        in_specs=[row(Wv), row(Ws), pl.BlockSpec((tm, Ws), lambda i: (i, u_col)), full(1, Ws),
                  full(Ws, Ws), full(1, Ws), full(Wv, D), full(Ws, D), row(D)],
        out_specs=row(D),
        out_shape=jax.ShapeDtypeStruct((T, D), F32),
        compiler_params=_cparams("parallel"),
        name="ab_out",
    )(ret, y, proj, d_skip.reshape(1, Ws), w_glu.astype(BF16), b_glu.reshape(1, Ws),
      w_out[:Wv].astype(BF16), w_out[Wv:].astype(BF16), h)


def _attn_kernel(q_ref, k_ref, v_ref, o_ref, ob_ref, lse_ref, *, L, branches):
    scale = ATT_DH ** -0.5

    def block(bi, d, band, qstart, kstart, has_prev):
        width = 2 * band if has_prev else band
        q = (q_ref[pl.ds(qstart, band, stride=d), :] * scale).astype(BF16)
        kw = k_ref[pl.ds(kstart, width, stride=d), :].astype(BF16)
        vw = v_ref[pl.ds(kstart, width, stride=d), :].astype(BF16)
        s = lax.dot_general(q, kw, (((1,), (1,)), ((), ())), preferred_element_type=F32)
        qi = lax.broadcasted_iota(jnp.int32, (band, width), 0)
        kj = lax.broadcasted_iota(jnp.int32, (band, width), 1)
        if has_prev:
            dist = band + qi - kj
            valid = (dist >= 0) & (dist <= band)
        else:
            valid = kj <= qi
        s = jnp.where(valid, s, NEG_BIG)
        m = jnp.max(s, axis=-1, keepdims=True)
        p = jnp.exp(s - m)
        l = jnp.sum(p, axis=-1, keepdims=True)
        o = jnp.dot(p.astype(BF16), vw, preferred_element_type=F32) / l
        ob_ref[bi, pl.ds(qstart, band, stride=d), :] = o
        lse_ref[bi, pl.ds(qstart, band, stride=d), :] = m + jnp.log(l)

    for bi, (window, d) in enumerate(branches):
        band = window // d
        nb = (L // d) // band

        def residue(r, carry, bi=bi, d=d, band=band, nb=nb):
            block(bi, d, band, r, r, False)

            def nblock(n, c):
                block(bi, d, band, r + d * band * n, r + d * band * (n - 1), True)
                return c

            lax.fori_loop(1, nb, nblock, 0)
            return carry

        lax.fori_loop(0, d, residue, 0)

    tile = 512
    nbr = len(branches)

    def merge(i, carry):
        rows = pl.ds(pl.multiple_of(i * tile, tile), tile)
        lses = [lse_ref[b, rows, :] for b in range(nbr)]
        mx = functools.reduce(jnp.maximum, lses)
        ws = [jnp.exp(x - mx) for x in lses]
        tot = functools.reduce(lambda a, b: a + b, ws)
        acc = (ws[0] / tot) * ob_ref[0, rows, :]
        for b in range(1, nbr):
            acc = acc + (ws[b] / tot) * ob_ref[b, rows, :]
        o_ref[rows, :] = acc
        return carry

    lax.fori_loop(0, L // tile, merge, 0)


def dilated_attention(qkv, B, L):
    H, dh = ATT_HEADS, ATT_DH
    nbr = len(DILATED_BRANCHES)

    def col(off):
        return pl.BlockSpec((L, dh), lambda b, h: (b, off + h))

    return pl.pallas_call(
        functools.partial(_attn_kernel, L=L, branches=DILATED_BRANCHES),
        grid=(B, H),
        in_specs=[col(0), col(H), col(2 * H)],
        out_specs=pl.BlockSpec((L, dh), lambda b, h: (b, h)),
        out_shape=jax.ShapeDtypeStruct((B * L, H * dh), F32),
        scratch_shapes=[pltpu.VMEM((nbr, L, dh), F32), pltpu.VMEM((nbr, L, 1), F32)],
        compiler_params=_cparams("parallel", "parallel"),
        name="dilated_attention",
    )(qkv, qkv, qkv)


def _router_kernel(h_ref, g_ref, whi_ref, wlo_ref, bias_ref, xn_ref, comb_ref):
    xn = _rms(h_ref[...], g_ref[...])
    xhi = xn.astype(BF16)
    xlo = (xn - xhi.astype(F32)).astype(BF16)
    xn_ref[...] = xhi
    whi = whi_ref[...]
    logits = (jnp.dot(xhi, whi, preferred_element_type=F32) + jnp.dot(xlo, whi, preferred_element_type=F32)
              + jnp.dot(xhi, wlo_ref[...], preferred_element_type=F32) + bias_ref[...])
    shape = logits.shape
    lane = lax.broadcasted_iota(jnp.int32, shape, 1)
    lanef = lane.astype(F32)
    far = float(LANES)

    def first_argmax(vals):
        mx = jnp.max(vals, axis=-1, keepdims=True)
        idx = jnp.min(jnp.where(vals == mx, lanef, far), axis=-1, keepdims=True)
        return mx, idx

    isg = lane < N_GROUPS
    gl = jnp.where(isg, logits, NEG_BIG)
    gmax, gsel = first_argmax(gl)
    gval = 1.0 / jnp.sum(jnp.where(isg, jnp.exp(gl - gmax), 0.0), axis=-1, keepdims=True)
    egroup = ((lane - N_GROUPS) >> 2).astype(F32)
    ise = (lane >= N_GROUPS) & (lane < N_GROUPS + N_EXPERTS) & (egroup == gsel)
    el = jnp.where(ise, logits, NEG_BIG)
    v1, i1 = first_argmax(el)
    el2 = jnp.where(lanef == i1, NEG_BIG, el)
    v2, i2 = first_argmax(el2)
    e21 = jnp.exp(v2 - v1)
    w1 = gval / (1.0 + e21)
    w2 = w1 * e21
    comb_ref[...] = jnp.where(lanef == i1, w1, 0.0) + jnp.where(lanef == i2, w2, 0.0)


def moe_router(h, g, w_group, b_group, w_router, b_router, tm=512):
    T, D = h.shape
    wcat = jnp.concatenate([w_group, w_router.reshape(D, N_EXPERTS)], axis=1)
    wcat = jnp.pad(wcat, ((0, 0), (0, LANES - wcat.shape[1])))
    whi = wcat.astype(BF16)
    wlo = (wcat - whi.astype(F32)).astype(BF16)
    bias = jnp.pad(jnp.concatenate([b_group, b_router.reshape(N_EXPERTS)]), (0, LANES - N_GROUPS - N_EXPERTS))
    return pl.pallas_call(
        _router_kernel,
        grid=(T // tm,),
        in_specs=[pl.BlockSpec((tm, D), lambda i: (i, 0)),
                  pl.BlockSpec((1, D), lambda i: (0, 0)),
                  pl.BlockSpec((D, LANES), lambda i: (0, 0)),
                  pl.BlockSpec((D, LANES), lambda i: (0, 0)),
                  pl.BlockSpec((1, LANES), lambda i: (0, 0))],
        out_specs=[pl.BlockSpec((tm, D), lambda i: (i, 0)),
                   pl.BlockSpec((tm, LANES), lambda i: (i, 0))],
        out_shape=[jax.ShapeDtypeStruct((T, D), BF16), jax.ShapeDtypeStruct((T, LANES), F32)],
        compiler_params=_cparams("parallel"),
        name="moe_router",
    )(h, g.reshape(1, D), whi, wlo, bias.reshape(1, LANES))


def _moe_dense_kernel(x_ref, comb_ref, wg_ref, wu_ref, wd_ref, h_ref, fn_ref, o_ref, acc_ref, *, final_norm):
    e = pl.program_id(1)

    @pl.when(e == 0)
    def _():
        acc_ref[...] = jnp.zeros_like(acc_ref)

    comb = comb_ref[...]
    lane = lax.broadcasted_iota(jnp.int32, comb.shape, 1)
    c = jnp.sum(jnp.where(lane == e + N_GROUPS, comb, 0.0), axis=-1, keepdims=True)
    x = x_ref[...]
    hg = jnp.dot(x, wg_ref[...], preferred_element_type=F32)
    hu = jnp.dot(x, wu_ref[...], preferred_element_type=F32)
    hh = hg * jax.nn.sigmoid(hg) * hu * c
    acc_ref[...] += jnp.dot(hh.astype(BF16), wd_ref[...], preferred_element_type=F32)

    @pl.when(e == pl.num_programs(1) - 1)
    def _():
        out = h_ref[...] + acc_ref[...]
        if final_norm:
            out = _rms(out, fn_ref[...])
        o_ref[...] = out


def moe_dense(xn, comb, w_gate, w_up, w_down, h, fnorm, final_norm, tm=1024):
    T, D = h.shape
    E, _, Fd = w_gate.shape
    tm = min(tm, T)
    return pl.pallas_call(
        functools.partial(_moe_dense_kernel, final_norm=final_norm),
        grid=(T // tm, E),
        in_specs=[pl.BlockSpec((tm, D), lambda i, e: (i, 0)),
                  pl.BlockSpec((tm, LANES), lambda i, e: (i, 0)),
                  pl.BlockSpec((None, D, Fd), lambda i, e: (e, 0, 0)),
                  pl.BlockSpec((None, D, Fd), lambda i, e: (e, 0, 0)),
                  pl.BlockSpec((None, Fd, D), lambda i, e: (e, 0, 0)),
                  pl.BlockSpec((tm, D), lambda i, e: (i, 0)),
                  pl.BlockSpec((1, D), lambda i, e: (0, 0))],
        out_specs=pl.BlockSpec((tm, D), lambda i, e: (i, 0)),
        out_shape=jax.ShapeDtypeStruct((T, D), F32),
        scratch_shapes=[pltpu.VMEM((tm, D), F32)],
        compiler_params=_cparams("parallel", "arbitrary"),
        name="moe_dense",
    )(xn, comb, w_gate, w_up, w_down, h, fnorm.reshape(1, D))


def hier_moe(h, g, w_group, b_group, w_router, b_router, w_gate, w_up, w_down, fnorm, final_norm):
    D = h.shape[1]
    xn, comb = moe_router(h, g, w_group, b_group, w_router, b_router)
    Fd = w_gate.shape[-1]
    return moe_dense(xn, comb, w_gate.reshape(N_EXPERTS, D, Fd).astype(BF16),
                     w_up.reshape(N_EXPERTS, D, Fd).astype(BF16),
                     w_down.reshape(N_EXPERTS, Fd, D).astype(BF16), h, fnorm, final_norm)


def kernel(x, mix_norm, ffn_norm, final_norm, ab_w_in, ab_w_out, s5_a_re, s5_a_im, s5_b_re, s5_b_im,
           s5_c_re, s5_c_im, s5_d, s5_log_dt, s5_w_glu, s5_b_glu, c_w_qkv, c_w_out,
           moe_w_group, moe_b_group, moe_w_router, moe_b_router, moe_w_gate, moe_w_up, moe_w_down):
    B, L, D = x.shape
    h = x.reshape(B * L, D)
    depth = mix_norm.shape[0]
    for layer in range(depth):
        i = layer // 2
        if layer % 2 == 0:
            proj = norm_matmul(h, mix_norm[layer], ab_w_in[i].astype(BF16))
            ret = retention_mixer(proj, B, L)
            s5_w = s5_w_glu.shape[-1]
            y = s5_mixer(proj[:, -s5_w:], B, L, s5_a_re[i], s5_a_im[i], s5_b_re[i], s5_b_im[i],
                         s5_c_re[i], s5_c_im[i], s5_log_dt[i])
            h = ab_out(ret, y, proj, s5_d[i].reshape(-1), s5_w_glu[i], s5_b_glu[i], ab_w_out[i], h)
        else:
            qkv = norm_matmul(h, mix_norm[layer], c_w_qkv[i].astype(BF16))
            att = dilated_attention(qkv, B, L)
            h = matmul_residual(att, c_w_out[i].astype(BF16), h)
        h = hier_moe(h, ffn_norm[layer], moe_w_group[layer], moe_b_group[layer], moe_w_router[layer],
                     moe_b_router[layer], moe_w_gate[layer], moe_w_up[layer], moe_w_down[layer],
                     final_norm, layer == depth - 1)
    return h.reshape(B, L, D)
```

```python
import functools
import math

import jax
import jax.numpy as jnp
from jax import lax
from jax.experimental import pallas as pl
from jax.experimental.pallas import tpu as pltpu

F32 = jnp.float32
BF16 = jnp.bfloat16

RET_HEADS = 4
RET_DK = 128
ROPE_BASE = 10000.0
S5_GROUP = 16
S5_STATE = 64
ATT_HEADS = 8
ATT_DH = 128
DILATED_BRANCHES = ((128, 1), (512, 4), (2048, 16))
N_GROUPS = 4
EXPERTS_PER_GROUP = 4
N_EXPERTS = N_GROUPS * EXPERTS_PER_GROUP
RMS_EPS = 1e-6
GN_EPS = 1e-6

LANES = 128
VMEM_LIMIT = 48 * 1024 * 1024
S5_CHUNK = 16
RET_CHUNK = 256
NEG_BIG = -1e30


def _cparams(*sem):
    return pltpu.CompilerParams(dimension_semantics=sem, vmem_limit_bytes=VMEM_LIMIT)


def _rms(x, g):
    return x * lax.rsqrt(jnp.mean(x * x, axis=-1, keepdims=True) + RMS_EPS) * g


def _norm_matmul_kernel(x_ref, g_ref, w_ref, o_ref):
    y = _rms(x_ref[...], g_ref[...])
    o_ref[...] = jnp.dot(y.astype(BF16), w_ref[...], preferred_element_type=F32)


def norm_matmul(x, g, w_bf16, tm=256):
    T, D = x.shape
    N = w_bf16.shape[1]
    return pl.pallas_call(
        _norm_matmul_kernel,
        grid=(T // tm,),
        in_specs=[pl.BlockSpec((tm, D), lambda i: (i, 0)),
                  pl.BlockSpec((1, D), lambda i: (0, 0)),
                  pl.BlockSpec((D, N), lambda i: (0, 0))],
        out_specs=pl.BlockSpec((tm, N), lambda i: (i, 0)),
        out_shape=jax.ShapeDtypeStruct((T, N), F32),
        compiler_params=_cparams("parallel"),
        name="norm_matmul",
    )(x, g.reshape(1, D), w_bf16)


def _matmul_res_kernel(a_ref, w_ref, r_ref, o_ref):
    o_ref[...] = r_ref[...] + jnp.dot(a_ref[...].astype(BF16), w_ref[...], preferred_element_type=F32)


def matmul_residual(a, w_bf16, res, tm=256):
    T, K = a.shape
    N = w_bf16.shape[1]
    return pl.pallas_call(
        _matmul_res_kernel,
        grid=(T // tm,),
        in_specs=[pl.BlockSpec((tm, K), lambda i: (i, 0)),
                  pl.BlockSpec((K, N), lambda i: (0, 0)),
                  pl.BlockSpec((tm, N), lambda i: (i, 0))],
        out_specs=pl.BlockSpec((tm, N), lambda i: (i, 0)),
        out_shape=jax.ShapeDtypeStruct((T, N), F32),
        compiler_params=_cparams("parallel"),
        name="matmul_residual",
    )(a, w_bf16, res)


def _retention_kernel(q_ref, k_ref, v_ref, gate_ref, cos_ref, sin_ref, decay_ref, xi_ref, zeta_ref, cd_ref,
                      o_ref, state_ref, *, n_chunks, chunk):
    state_ref[...] = jnp.zeros_like(state_ref)
    decay = decay_ref[...]
    xi = xi_ref[...]
    zeta = zeta_ref[...]
    cd = cd_ref[...]
    half = RET_DK // 2

    def rot(x, cos, sin):
        return x * cos + pltpu.roll(x, half, axis=1) * sin

    def body(n, carry):
        rows = pl.ds(pl.multiple_of(n * chunk, chunk), chunk)
        cos = cos_ref[rows, :]
        sin = sin_ref[rows, :]
        q = rot(q_ref[rows, :], cos, sin)
        k = rot(k_ref[rows, :], cos, sin) * (RET_DK ** -0.5)
        v = v_ref[rows, :].astype(BF16)
        s = lax.dot_general(q.astype(BF16), k.astype(BF16), (((1,), (1,)), ((), ())),
                            preferred_element_type=F32) * decay
        state = state_ref[...]
        o = jnp.dot(s.astype(BF16), v, preferred_element_type=F32)
        o = o + jnp.dot((q * xi).astype(BF16), state.astype(BF16), preferred_element_type=F32)
        kv = lax.dot_general((k * zeta).astype(BF16), v, (((0,), (0,)), ((), ())),
                             preferred_element_type=F32)
        state_ref[...] = state * cd + kv
        mu = jnp.mean(o, axis=-1, keepdims=True)
        oc = o - mu
        var = jnp.mean(oc * oc, axis=-1, keepdims=True)
        on = oc * lax.rsqrt(var + GN_EPS)
        g = gate_ref[rows, :]
        o_ref[rows, :] = g * jax.nn.sigmoid(g) * on
        return carry

    lax.fori_loop(0, n_chunks, body, 0)


def retention_mixer(proj, B, L, chunk=RET_CHUNK):
    H, dk = RET_HEADS, RET_DK
    chunk = min(chunk, L)
    half = dk // 2
    pos = jnp.arange(L, dtype=F32)
    inv = ROPE_BASE ** (-jnp.arange(half, dtype=F32) / half)
    ang = pos[:, None] * inv[None, :]
    cos = jnp.concatenate([jnp.cos(ang), jnp.cos(ang)], axis=-1)
    sin = jnp.concatenate([-jnp.sin(ang), jnp.sin(ang)], axis=-1)
    lg = jnp.log1p(-jnp.exp2(-5.0 - jnp.arange(H, dtype=F32)))
    idx = jnp.arange(chunk, dtype=F32)
    diff = idx[:, None] - idx[None, :]
    decay = jnp.where(diff[None] >= 0, jnp.exp(jnp.maximum(diff, 0.0)[None] * lg[:, None, None]), 0.0)
    xi = jnp.exp((idx + 1.0)[None, :] * lg[:, None])[..., None]
    zeta = jnp.exp((chunk - 1.0 - idx)[None, :] * lg[:, None])[..., None]
    cd = jnp.exp(chunk * lg)[:, None, None]

    def col(off):
        return pl.BlockSpec((L, dk), lambda b, h: (b, off + h))

    tab = pl.BlockSpec((L, dk), lambda b, h: (0, 0))
    return pl.pallas_call(
        functools.partial(_retention_kernel, n_chunks=L // chunk, chunk=chunk),
        grid=(B, H),
        in_specs=[col(0), col(H), col(2 * H), col(3 * H), tab, tab,
                  pl.BlockSpec((None, chunk, chunk), lambda b, h: (h, 0, 0)),
                  pl.BlockSpec((None, chunk, 1), lambda b, h: (h, 0, 0)),
                  pl.BlockSpec((None, chunk, 1), lambda b, h: (h, 0, 0)),
                  pl.BlockSpec((None, 1, 1), lambda b, h: (h, 0, 0))],
        out_specs=pl.BlockSpec((L, dk), lambda b, h: (b, h)),
        out_shape=jax.ShapeDtypeStruct((B * L, H * dk), F32),
        scratch_shapes=[pltpu.VMEM((dk, dk), F32)],
        compiler_params=_cparams("parallel", "parallel"),
        name="retention",
    )(proj, proj, proj, proj, cos, sin, decay, xi, zeta, cd)


def _s5_kernel(*refs, G, batch, n_chunks):
    ncb = G * S5_GROUP // LANES
    u_refs = refs[:ncb]
    (kt_ref, bre_ref, bim_ref, cre_ref, cim_ref, lre_ref, lim_ref, y_ref,
     ug_ref, vre_ref, vim_ref, sre_ref, sim_ref, yg_ref, st_ref) = refs[ncb:]
    C, Hs, P = S5_CHUNK, S5_GROUP, S5_STATE
    W = C * Hs
    gpl = LANES // Hs
    spl = LANES // Hs
    rb = 16

    @pl.when(pl.program_id(0) == 0)
    def _():
        st_ref[...] = jnp.zeros_like(st_ref)

    def pack(i, carry):
        b = i // (n_chunks // rb)
        n0 = (i % (n_chunks // rb)) * rb
        r0 = pl.multiple_of(b * n_chunks + n0, rb)
        for cb in range(G // gpl):
            xs = [u_refs[cb][b, pl.ds(s + C * n0, rb, stride=C), :] for s in range(C)]
            for gl in range(gpl):
                for half in range(C // spl):
                    piece = jnp.concatenate(
                        [xs[half * spl + k][:, gl * Hs:(gl + 1) * Hs] for k in range(spl)], axis=1)
                    ug_ref[cb * gpl + gl, pl.ds(r0, rb), half * LANES:(half + 1) * LANES] = piece.astype(BF16)
        return carry

    lax.fori_loop(0, batch * n_chunks // rb, pack, 0)

    for g in range(G):
        u = ug_ref[g]
        vre_ref[:, g * P:(g + 1) * P] = jnp.dot(u, bre_ref[g], preferred_element_type=F32)
        vim_ref[:, g * P:(g + 1) * P] = jnp.dot(u, bim_ref[g], preferred_element_type=F32)

    lre = lre_ref[...]
    lim = lim_ref[...]

    def step(n, carry):
        new = []
        for b in range(batch):
            sr, si = carry[2 * b], carry[2 * b + 1]
            row = pl.ds(b * n_chunks + n, 1)
            sre_ref[row, :] = sr
            sim_ref[row, :] = si
            new.append(sr * lre - si * lim + vre_ref[row, :])
            new.append(sr * lim + si * lre + vim_ref[row, :])
        return tuple(new)

    init = tuple(st_ref[k, pl.ds(b, 1), :] for b in range(batch) for k in range(2))
    fin = lax.fori_loop(0, n_chunks, step, init)
    for b in range(batch):
        st_ref[0, pl.ds(b, 1), :] = fin[2 * b]
        st_ref[1, pl.ds(b, 1), :] = fin[2 * b + 1]

    for g in range(G):
        y = jnp.dot(ug_ref[g], kt_ref[g], preferred_element_type=F32)
        y = y + jnp.dot(sre_ref[:, g * P:(g + 1) * P].astype(BF16), cre_ref[g], preferred_element_type=F32)
        y = y + jnp.dot(sim_ref[:, g * P:(g + 1) * P].astype(BF16), cim_ref[g], preferred_element_type=F32)
        yg_ref[g] = y

    def unpack(i, carry):
        b = i // (n_chunks // 8)
        n0 = (i % (n_chunks // 8)) * 8
        r0 = pl.multiple_of(b * n_chunks + n0, 8)
        for cb in range(G // gpl):
            for s in range(C):
                piece = jnp.concatenate(
                    [yg_ref[cb * gpl + gl, pl.ds(r0, 8), s * Hs:(s + 1) * Hs] for gl in range(gpl)], axis=1)
                y_ref[cb, b, pl.ds(s + C * n0, 8, stride=C), :] = piece
        return carry

    lax.fori_loop(0, batch * n_chunks // 8, unpack, 0)


def _s5_tables(a_re, a_im, b_re, b_im, c_re, c_im, log_dt):
    C = S5_CHUNK
    G, P = a_re.shape
    Hs = S5_GROUP
    dt = jnp.exp(log_dt)[:, None]
    zr, zi = a_re * dt, a_im * dt
    mag = jnp.exp(zr)
    lam_re, lam_im = mag * jnp.cos(zi), mag * jnp.sin(zi)
    nr, ni = lam_re - 1.0, lam_im
    den = a_re * a_re + a_im * a_im
    coef_re = (nr * a_re + ni * a_im) / den
    coef_im = (ni * a_re - nr * a_im) / den
    bb_re = coef_re[..., None] * b_re - coef_im[..., None] * b_im
    bb_im = coef_re[..., None] * b_im + coef_im[..., None] * b_re
    tau = jnp.arange(C + 1, dtype=F32)[:, None, None]
    pmag = jnp.exp(tau * zr[None])
    pw_re, pw_im = pmag * jnp.cos(tau * zi[None]), pmag * jnp.sin(tau * zi[None])
    lb_re = pw_re[..., None] * bb_re[None] - pw_im[..., None] * bb_im[None]
    lb_im = pw_re[..., None] * bb_im[None] + pw_im[..., None] * bb_re[None]
    kern = (jnp.einsum('ghp,tgpk->tghk', c_re, lb_re[:C]) - jnp.einsum('ghp,tgpk->tghk', c_im, lb_im[:C]))
    s = jnp.arange(C)
    lag = s[None, :] - s[:, None]
    kt = jnp.where((lag >= 0)[:, :, None, None, None], kern[jnp.clip(lag, 0, C - 1)], 0.0)
    kt = kt.transpose(2, 0, 4, 1, 3).reshape(G, C * Hs, C * Hs)
    bst_re = lb_re[:C][::-1].transpose(1, 0, 3, 2).reshape(G, C * Hs, P)
    bst_im = lb_im[:C][::-1].transpose(1, 0, 3, 2).reshape(G, C * Hs, P)
    cl_re = c_re[None] * pw_re[1:, :, None, :] - c_im[None] * pw_im[1:, :, None, :]
    cl_im = c_re[None] * pw_im[1:, :, None, :] + c_im[None] * pw_re[1:, :, None, :]
    cst_re = cl_re.transpose(1, 3, 0, 2).reshape(G, P, C * Hs)
    cst_im = (-cl_im).transpose(1, 3, 0, 2).reshape(G, P, C * Hs)
    return (kt.astype(BF16), bst_re.astype(BF16), bst_im.astype(BF16), cst_re.astype(BF16), cst_im.astype(BF16),
            pw_re[C], pw_im[C])


def s5_mixer(proj, B, L, a_re, a_im, b_re, b_im, c_re, c_im, log_dt, block=512):
    C, Hs, P = S5_CHUNK, S5_GROUP, S5_STATE
    G = a_re.shape[0]
    Wu = G * Hs
    block = min(block, L)
    nc = block // C
    R = B * nc
    W = C * Hs
    kt, bst_re, bst_im, cst_re, cst_im, l_re, l_im = _s5_tables(a_re, a_im, b_re, b_im, c_re, c_im, log_dt)
    ncb = Wu // LANES
    u_col = (proj.shape[1] - Wu) // LANES

    def table(*shape):
        return pl.BlockSpec(shape, lambda j: (0,) * len(shape), pipeline_mode=pl.Buffered(1))

    proj3 = proj.reshape(B, L, proj.shape[1])
    y = pl.pallas_call(
        functools.partial(_s5_kernel, G=G, batch=B, n_chunks=nc),
        grid=(L // block,),
        in_specs=[pl.BlockSpec((B, block, LANES), lambda j, cb=cb: (0, j, u_col + cb)) for cb in range(ncb)]
                 + [table(G, W, W), table(G, W, P), table(G, W, P), table(G, P, W), table(G, P, W),
                    table(1, G * P), table(1, G * P)],
        out_specs=pl.BlockSpec((ncb, B, block, LANES), lambda j: (0, 0, j, 0)),
        out_shape=jax.ShapeDtypeStruct((ncb, B, L, LANES), F32),
        scratch_shapes=[pltpu.VMEM((G, R, W), BF16),
                        pltpu.VMEM((R, G * P), F32), pltpu.VMEM((R, G * P), F32),
                        pltpu.VMEM((R, G * P), F32), pltpu.VMEM((R, G * P), F32),
                        pltpu.VMEM((G, R, W), F32),
                        pltpu.VMEM((2, B, G * P), F32)],
        compiler_params=_cparams("arbitrary"),
        name="s5_chunked",
    )(*([proj3] * ncb), kt, bst_re, bst_im, cst_re, cst_im,
      l_re.reshape(1, G * P), l_im.reshape(1, G * P))
    return y.reshape(ncb, B * L, LANES)


def _ab_out_kernel(ret_ref, y_ref, u_ref, d_ref, wglu_ref, bglu_ref, wo1_ref, wo2_ref, h_ref, o_ref):
    y = jnp.concatenate([y_ref[cb] for cb in range(y_ref.shape[0])], axis=1) + d_ref[...] * u_ref[...]
    g = jax.nn.gelu(y)
    z = jnp.dot(g.astype(BF16), wglu_ref[...], preferred_element_type=F32) + bglu_ref[...]
    s5 = g * jax.nn.sigmoid(z)
    acc = jnp.dot(ret_ref[...].astype(BF16), wo1_ref[...], preferred_element_type=F32)
    acc = acc + jnp.dot(s5.astype(BF16), wo2_ref[...], preferred_element_type=F32)
    o_ref[...] = h_ref[...] + acc


def ab_out(ret, y, proj, d_skip, w_glu, b_glu, w_out, h, tm=256):
    T, Wv = ret.shape
    ncb = y.shape[0]
    Ws = ncb * y.shape[2]
    D = h.shape[1]
    u_col = (proj.shape[1] - Ws) // Ws
    row = lambda w: pl.BlockSpec((tm, w), lambda i: (i, 0))
    full = lambda a, b: pl.BlockSpec((a, b), lambda i: (0, 0))
    return pl.pallas_call(
        _ab_out_kernel,
        grid=(T // tm,),
        in_specs=[row(Wv), pl.BlockSpec((ncb, tm, y.shape[2]), lambda i: (0, i, 0)),
                  pl.BlockSpec((tm, Ws), lambda i: (i, u_col)), full(1, Ws),
                  full(Ws, Ws), full(1, Ws), full(Wv, D), full(Ws, D), row(D)],
        out_specs=row(D),
        out_shape=jax.ShapeDtypeStruct((T, D), F32),
        compiler_params=_cparams("parallel"),
        name="ab_out",
    )(ret, y, proj, d_skip.reshape(1, Ws), w_glu.astype(BF16), b_glu.reshape(1, Ws),
      w_out[:Wv].astype(BF16), w_out[Wv:].astype(BF16), h)


def _attn_kernel(q_ref, k_ref, v_ref, bias_ref, o_ref, ob_ref, lse_ref, *, L, branches, interleave):
    scale = ATT_DH ** -0.5

    def block(bi, d, band, r, n):
        qstart = r + d * band * n
        kstart = r + d * band * jnp.maximum(n - 1, 0)
        q = (q_ref[pl.ds(qstart, band, stride=d), :] * scale).astype(BF16)
        kw = k_ref[pl.ds(kstart, 2 * band, stride=d), :].astype(BF16)
        vw = v_ref[pl.ds(kstart, 2 * band, stride=d), :].astype(BF16)
        s = lax.dot_general(q, kw, (((1,), (1,)), ((), ())), preferred_element_type=F32)
        s = s + bias_ref[jnp.minimum(n, 1)]
        m = jnp.max(s, axis=-1, keepdims=True)
        p = jnp.exp(s - m)
        l = jnp.sum(p, axis=-1, keepdims=True)
        o = jnp.dot(p.astype(BF16), vw, preferred_element_type=F32) * (1.0 / l)
        ob_ref[bi, pl.ds(qstart, band, stride=d), :] = o
        lse_ref[bi, pl.ds(qstart, band, stride=d), :] = m + jnp.log(l)

    for bi, (window, d) in enumerate(branches):
        band = window // d
        nb = (L // d) // band
        assert nb >= 2 and (d * nb) % interleave == 0

        def body(j, carry, bi=bi, d=d, band=band, nb=nb):
            for u in range(interleave):
                jj = j * interleave + u
                block(bi, d, band, jj // nb, jj % nb)
            return carry

        lax.fori_loop(0, (d * nb) // interleave, body, 0)

    tile = 512
    nbr = len(branches)

    def merge(i, carry):
        rows = pl.ds(pl.multiple_of(i * tile, tile), tile)
        lses = [lse_ref[b, rows, :] for b in range(nbr)]
        mx = functools.reduce(jnp.maximum, lses)
        ws = [jnp.exp(x - mx) for x in lses]
        inv = 1.0 / functools.reduce(lambda a, b: a + b, ws)
        acc = (ws[0] * inv) * ob_ref[0, rows, :]
        for b in range(1, nbr):
            acc = acc + (ws[b] * inv) * ob_ref[b, rows, :]
        o_ref[rows, :] = acc
        return carry

    lax.fori_loop(0, L // tile, merge, 0)


def dilated_attention(qkv, B, L, interleave=4):
    H, dh = ATT_HEADS, ATT_DH
    nbr = len(DILATED_BRANCHES)
    band = DILATED_BRANCHES[0][0] // DILATED_BRANCHES[0][1]
    assert all(w // d == band for w, d in DILATED_BRANCHES)
    qi = jnp.arange(band)[:, None]
    kj = jnp.arange(2 * band)[None, :]
    dist = band + qi - kj
    bias = jnp.stack([jnp.where(kj <= qi, 0.0, NEG_BIG),
                      jnp.where((dist >= 0) & (dist <= band), 0.0, NEG_BIG)]).astype(F32)

    def col(off):
        return pl.BlockSpec((L, dh), lambda b, h: (b, off + h))

    return pl.pallas_call(
        functools.partial(_attn_kernel, L=L, branches=DILATED_BRANCHES, interleave=interleave),
        grid=(B, H),
        in_specs=[col(0), col(H), col(2 * H),
                  pl.BlockSpec((2, band, 2 * band), lambda b, h: (0, 0, 0))],
        out_specs=pl.BlockSpec((L, dh), lambda b, h: (b, h)),
        out_shape=jax.ShapeDtypeStruct((B * L, H * dh), F32),
        scratch_shapes=[pltpu.VMEM((nbr, L, dh), F32), pltpu.VMEM((nbr, L, 1), F32)],
        compiler_params=_cparams("parallel", "parallel"),
        name="dilated_attention",
    )(qkv, qkv, qkv, bias)


def _router_kernel(h_ref, g_ref, whi_ref, wlo_ref, bias_ref, xn_ref, comb_ref):
    xn = _rms(h_ref[...], g_ref[...])
    xhi = xn.astype(BF16)
    xlo = (xn - xhi.astype(F32)).astype(BF16)
    xn_ref[...] = xhi
    whi = whi_ref[...]
    logits = (jnp.dot(xhi, whi, preferred_element_type=F32) + jnp.dot(xlo, whi, preferred_element_type=F32)
              + jnp.dot(xhi, wlo_ref[...], preferred_element_type=F32) + bias_ref[...])
    shape = logits.shape
    lane = lax.broadcasted_iota(jnp.int32, shape, 1)
    lanef = lane.astype(F32)
    far = float(LANES)

    def first_argmax(vals):
        mx = jnp.max(vals, axis=-1, keepdims=True)
        idx = jnp.min(jnp.where(vals == mx, lanef, far), axis=-1, keepdims=True)
        return mx, idx

    isg = lane < N_GROUPS
    gl = jnp.where(isg, logits, NEG_BIG)
    gmax, gsel = first_argmax(gl)
    gval = 1.0 / jnp.sum(jnp.where(isg, jnp.exp(gl - gmax), 0.0), axis=-1, keepdims=True)
    egroup = ((lane - N_GROUPS) >> 2).astype(F32)
    ise = (lane >= N_GROUPS) & (lane < N_GROUPS + N_EXPERTS) & (egroup == gsel)
    el = jnp.where(ise, logits, NEG_BIG)
    v1, i1 = first_argmax(el)
    el2 = jnp.where(lanef == i1, NEG_BIG, el)
    v2, i2 = first_argmax(el2)
    e21 = jnp.exp(v2 - v1)
    w1 = gval / (1.0 + e21)
    w2 = w1 * e21
    comb_ref[...] = jnp.where(lanef == i1, w1, 0.0) + jnp.where(lanef == i2, w2, 0.0)


def moe_router(h, g, w_group, b_group, w_router, b_router, tm=512):
    T, D = h.shape
    wcat = jnp.concatenate([w_group, w_router.reshape(D, N_EXPERTS)], axis=1)
    wcat = jnp.pad(wcat, ((0, 0), (0, LANES - wcat.shape[1])))
    whi = wcat.astype(BF16)
    wlo = (wcat - whi.astype(F32)).astype(BF16)
    bias = jnp.pad(jnp.concatenate([b_group, b_router.reshape(N_EXPERTS)]), (0, LANES - N_GROUPS - N_EXPERTS))
    return pl.pallas_call(
        _router_kernel,
        grid=(T // tm,),
        in_specs=[pl.BlockSpec((tm, D), lambda i: (i, 0)),
                  pl.BlockSpec((1, D), lambda i: (0, 0)),
                  pl.BlockSpec((D, LANES), lambda i: (0, 0)),
                  pl.BlockSpec((D, LANES), lambda i: (0, 0)),
                  pl.BlockSpec((1, LANES), lambda i: (0, 0))],
        out_specs=[pl.BlockSpec((tm, D), lambda i: (i, 0)),
                   pl.BlockSpec((tm, LANES), lambda i: (i, 0))],
        out_shape=[jax.ShapeDtypeStruct((T, D), BF16), jax.ShapeDtypeStruct((T, LANES), F32)],
        compiler_params=_cparams("parallel"),
        name="moe_router",
    )(h, g.reshape(1, D), whi, wlo, bias.reshape(1, LANES))


def _moe_dense_kernel(x_ref, comb_ref, wg_ref, wu_ref, wd_ref, h_ref, fn_ref, o_ref, acc_ref, *, final_norm):
    e = pl.program_id(1)

    @pl.when(e == 0)
    def _():
        acc_ref[...] = jnp.zeros_like(acc_ref)

    comb = comb_ref[...]
    lane = lax.broadcasted_iota(jnp.int32, comb.shape, 1)
    c = jnp.sum(jnp.where(lane == e + N_GROUPS, comb, 0.0), axis=-1, keepdims=True)
    x = x_ref[...]
    hg = jnp.dot(x, wg_ref[...], preferred_element_type=F32)
    hu = jnp.dot(x, wu_ref[...], preferred_element_type=F32)
    hh = hg * jax.nn.sigmoid(hg) * hu * c
    acc_ref[...] += jnp.dot(hh.astype(BF16), wd_ref[...], preferred_element_type=F32)

    @pl.when(e == pl.num_programs(1) - 1)
    def _():
        out = h_ref[...] + acc_ref[...]
        if final_norm:
            out = _rms(out, fn_ref[...])
        o_ref[...] = out


def moe_dense(xn, comb, w_gate, w_up, w_down, h, fnorm, final_norm, tm=1024):
    T, D = h.shape
    E, _, Fd = w_gate.shape
    tm = min(tm, T)
    return pl.pallas_call(
        functools.partial(_moe_dense_kernel, final_norm=final_norm),
        grid=(T // tm, E),
        in_specs=[pl.BlockSpec((tm, D), lambda i, e: (i, 0)),
                  pl.BlockSpec((tm, LANES), lambda i, e: (i, 0)),
                  pl.BlockSpec((None, D, Fd), lambda i, e: (e, 0, 0)),
                  pl.BlockSpec((None, D, Fd), lambda i, e: (e, 0, 0)),
                  pl.BlockSpec((None, Fd, D), lambda i, e: (e, 0, 0)),
                  pl.BlockSpec((tm, D), lambda i, e: (i, 0)),
                  pl.BlockSpec((1, D), lambda i, e: (0, 0))],
        out_specs=pl.BlockSpec((tm, D), lambda i, e: (i, 0)),
        out_shape=jax.ShapeDtypeStruct((T, D), F32),
        scratch_shapes=[pltpu.VMEM((tm, D), F32)],
        compiler_params=_cparams("parallel", "arbitrary"),
        name="moe_dense",
    )(xn, comb, w_gate, w_up, w_down, h, fnorm.reshape(1, D))


def hier_moe(h, g, w_group, b_group, w_router, b_router, w_gate, w_up, w_down, fnorm, final_norm):
    D = h.shape[1]
    xn, comb = moe_router(h, g, w_group, b_group, w_router, b_router)
    Fd = w_gate.shape[-1]
    return moe_dense(xn, comb, w_gate.reshape(N_EXPERTS, D, Fd).astype(BF16),
                     w_up.reshape(N_EXPERTS, D, Fd).astype(BF16),
                     w_down.reshape(N_EXPERTS, Fd, D).astype(BF16), h, fnorm, final_norm)


def kernel(x, mix_norm, ffn_norm, final_norm, ab_w_in, ab_w_out, s5_a_re, s5_a_im, s5_b_re, s5_b_im,
           s5_c_re, s5_c_im, s5_d, s5_log_dt, s5_w_glu, s5_b_glu, c_w_qkv, c_w_out,
           moe_w_group, moe_b_group, moe_w_router, moe_b_router, moe_w_gate, moe_w_up, moe_w_down):
    B, L, D = x.shape
    h = x.reshape(B * L, D)
    depth = mix_norm.shape[0]
    for layer in range(depth):
        i = layer // 2
        if layer % 2 == 0:
            proj = norm_matmul(h, mix_norm[layer], ab_w_in[i].astype(BF16))
            ret = retention_mixer(proj, B, L)
            y = s5_mixer(proj, B, L, s5_a_re[i], s5_a_im[i], s5_b_re[i], s5_b_im[i],
                         s5_c_re[i], s5_c_im[i], s5_log_dt[i])
            h = ab_out(ret, y, proj, s5_d[i].reshape(-1), s5_w_glu[i], s5_b_glu[i], ab_w_out[i], h)
        else:
            qkv = norm_matmul(h, mix_norm[layer], c_w_qkv[i].astype(BF16))
            att = dilated_attention(qkv, B, L)
            h = matmul_residual(att, c_w_out[i].astype(BF16), h)
        h = hier_moe(h, ffn_norm[layer], moe_w_group[layer], moe_b_group[layer], moe_w_router[layer],
                     moe_b_router[layer], moe_w_gate[layer], moe_w_up[layer], moe_w_down[layer],
                     final_norm, layer == depth - 1)
    return h.reshape(B, L, D)
```

```python
import functools
import math

import jax
import jax.numpy as jnp
from jax import lax
from jax.experimental import pallas as pl
from jax.experimental.pallas import tpu as pltpu

F32 = jnp.float32
BF16 = jnp.bfloat16

RET_HEADS = 4
RET_DK = 128
ROPE_BASE = 10000.0
S5_GROUP = 16
S5_STATE = 64
ATT_HEADS = 8
ATT_DH = 128
DILATED_BRANCHES = ((128, 1), (512, 4), (2048, 16))
N_GROUPS = 4
EXPERTS_PER_GROUP = 4
N_EXPERTS = N_GROUPS * EXPERTS_PER_GROUP
RMS_EPS = 1e-6
GN_EPS = 1e-6

LANES = 128
VMEM_LIMIT = 48 * 1024 * 1024
S5_CHUNK = 16
RET_CHUNK = 256
NEG_BIG = -1e30


def _cparams(*sem):
    return pltpu.CompilerParams(dimension_semantics=sem, vmem_limit_bytes=VMEM_LIMIT)


def _rms(x, g):
    return x * lax.rsqrt(jnp.mean(x * x, axis=-1, keepdims=True) + RMS_EPS) * g


def _norm_matmul_kernel(x_ref, g_ref, w_ref, o_ref):
    y = _rms(x_ref[...], g_ref[...])
    o_ref[...] = jnp.dot(y.astype(BF16), w_ref[...], preferred_element_type=F32)


def norm_matmul(x, g, w_bf16, tm=256):
    T, D = x.shape
    N = w_bf16.shape[1]
    return pl.pallas_call(
        _norm_matmul_kernel,
        grid=(T // tm,),
        in_specs=[pl.BlockSpec((tm, D), lambda i: (i, 0)),
                  pl.BlockSpec((1, D), lambda i: (0, 0)),
                  pl.BlockSpec((D, N), lambda i: (0, 0))],
        out_specs=pl.BlockSpec((tm, N), lambda i: (i, 0)),
        out_shape=jax.ShapeDtypeStruct((T, N), F32),
        compiler_params=_cparams("parallel"),
        name="norm_matmul",
    )(x, g.reshape(1, D), w_bf16)


def _matmul_res_kernel(a_ref, w_ref, r_ref, o_ref):
    o_ref[...] = r_ref[...] + jnp.dot(a_ref[...].astype(BF16), w_ref[...], preferred_element_type=F32)


def matmul_residual(a, w_bf16, res, tm=256):
    T, K = a.shape
    N = w_bf16.shape[1]
    return pl.pallas_call(
        _matmul_res_kernel,
        grid=(T // tm,),
        in_specs=[pl.BlockSpec((tm, K), lambda i: (i, 0)),
                  pl.BlockSpec((K, N), lambda i: (0, 0)),
                  pl.BlockSpec((tm, N), lambda i: (i, 0))],
        out_specs=pl.BlockSpec((tm, N), lambda i: (i, 0)),
        out_shape=jax.ShapeDtypeStruct((T, N), F32),
        compiler_params=_cparams("parallel"),
        name="matmul_residual",
    )(a, w_bf16, res)


def _retention_kernel(q_ref, k_ref, v_ref, gate_ref, cos_ref, sin_ref, decay_ref, xi_ref, zeta_ref, cd_ref,
                      o_ref, state_ref, *, n_chunks, chunk):
    state_ref[...] = jnp.zeros_like(state_ref)
    decay = decay_ref[...]
    xi = xi_ref[...]
    zeta = zeta_ref[...]
    cd = cd_ref[...]
    half = RET_DK // 2

    def rot(x, cos, sin):
        return x * cos + pltpu.roll(x, half, axis=1) * sin

    def body(n, carry):
        rows = pl.ds(pl.multiple_of(n * chunk, chunk), chunk)
        cos = cos_ref[rows, :]
        sin = sin_ref[rows, :]
        q = rot(q_ref[rows, :], cos, sin)
        k = rot(k_ref[rows, :], cos, sin) * (RET_DK ** -0.5)
        v = v_ref[rows, :].astype(BF16)
        s = lax.dot_general(q.astype(BF16), k.astype(BF16), (((1,), (1,)), ((), ())),
                            preferred_element_type=F32) * decay
        state = state_ref[...]
        o = jnp.dot(s.astype(BF16), v, preferred_element_type=F32)
        o = o + jnp.dot((q * xi).astype(BF16), state.astype(BF16), preferred_element_type=F32)
        kv = lax.dot_general((k * zeta).astype(BF16), v, (((0,), (0,)), ((), ())),
                             preferred_element_type=F32)
        state_ref[...] = state * cd + kv
        mu = jnp.mean(o, axis=-1, keepdims=True)
        oc = o - mu
        var = jnp.mean(oc * oc, axis=-1, keepdims=True)
        on = oc * lax.rsqrt(var + GN_EPS)
        g = gate_ref[rows, :]
        o_ref[rows, :] = g * jax.nn.sigmoid(g) * on
        return carry

    lax.fori_loop(0, n_chunks, body, 0)


def retention_mixer(proj, B, L, chunk=RET_CHUNK):
    H, dk = RET_HEADS, RET_DK
    chunk = min(chunk, L)
    half = dk // 2
    pos = jnp.arange(L, dtype=F32)
    inv = ROPE_BASE ** (-jnp.arange(half, dtype=F32) / half)
    ang = pos[:, None] * inv[None, :]
    cos = jnp.concatenate([jnp.cos(ang), jnp.cos(ang)], axis=-1)
    sin = jnp.concatenate([-jnp.sin(ang), jnp.sin(ang)], axis=-1)
    lg = jnp.log1p(-jnp.exp2(-5.0 - jnp.arange(H, dtype=F32)))
    idx = jnp.arange(chunk, dtype=F32)
    diff = idx[:, None] - idx[None, :]
    decay = jnp.where(diff[None] >= 0, jnp.exp(jnp.maximum(diff, 0.0)[None] * lg[:, None, None]), 0.0)
    xi = jnp.exp((idx + 1.0)[None, :] * lg[:, None])[..., None]
    zeta = jnp.exp((chunk - 1.0 - idx)[None, :] * lg[:, None])[..., None]
    cd = jnp.exp(chunk * lg)[:, None, None]

    def col(off):
        return pl.BlockSpec((L, dk), lambda b, h: (b, off + h))

    tab = pl.BlockSpec((L, dk), lambda b, h: (0, 0))
    return pl.pallas_call(
        functools.partial(_retention_kernel, n_chunks=L // chunk, chunk=chunk),
        grid=(B, H),
        in_specs=[col(0), col(H), col(2 * H), col(3 * H), tab, tab,
                  pl.BlockSpec((None, chunk, chunk), lambda b, h: (h, 0, 0)),
                  pl.BlockSpec((None, chunk, 1), lambda b, h: (h, 0, 0)),
                  pl.BlockSpec((None, chunk, 1), lambda b, h: (h, 0, 0)),
                  pl.BlockSpec((None, 1, 1), lambda b, h: (h, 0, 0))],
        out_specs=pl.BlockSpec((L, dk), lambda b, h: (b, h)),
        out_shape=jax.ShapeDtypeStruct((B * L, H * dk), F32),
        scratch_shapes=[pltpu.VMEM((dk, dk), F32)],
        compiler_params=_cparams("parallel", "parallel"),
        name="retention",
    )(proj, proj, proj, proj, cos, sin, decay, xi, zeta, cd)


def _s5_kernel(*refs, G, batch, n_chunks):
    ncb = G * S5_GROUP // LANES
    u_refs = refs[:ncb]
    (kt_ref, bre_ref, bim_ref, cre_ref, cim_ref, lre_ref, lim_ref, y_ref,
     ug_ref, vre_ref, vim_ref, sre_ref, sim_ref, yg_ref, st_ref) = refs[ncb:]
    C, Hs, P = S5_CHUNK, S5_GROUP, S5_STATE
    W = C * Hs
    gpl = LANES // Hs
    spl = LANES // Hs
    rb = 16

    @pl.when(pl.program_id(0) == 0)
    def _():
        st_ref[...] = jnp.zeros_like(st_ref)

    def pack(i, carry):
        b = i // (n_chunks // rb)
        n0 = (i % (n_chunks // rb)) * rb
        r0 = pl.multiple_of(b * n_chunks + n0, rb)
        for cb in range(G // gpl):
            xs = [u_refs[cb][b, pl.ds(s + C * n0, rb, stride=C), :] for s in range(C)]
            for gl in range(gpl):
                for half in range(C // spl):
                    piece = jnp.concatenate(
                        [xs[half * spl + k][:, gl * Hs:(gl + 1) * Hs] for k in range(spl)], axis=1)
                    ug_ref[cb * gpl + gl, pl.ds(r0, rb), half * LANES:(half + 1) * LANES] = piece.astype(BF16)
        return carry

    lax.fori_loop(0, batch * n_chunks // rb, pack, 0)

    for g in range(G):
        u = ug_ref[g]
        vre_ref[:, g * P:(g + 1) * P] = jnp.dot(u, bre_ref[g], preferred_element_type=F32)
        vim_ref[:, g * P:(g + 1) * P] = jnp.dot(u, bim_ref[g], preferred_element_type=F32)

    lre = lre_ref[...]
    lim = lim_ref[...]

    def step(n, carry):
        new = []
        for b in range(batch):
            sr, si = carry[2 * b], carry[2 * b + 1]
            row = pl.ds(b * n_chunks + n, 1)
            sre_ref[row, :] = sr
            sim_ref[row, :] = si
            new.append(sr * lre - si * lim + vre_ref[row, :])
            new.append(sr * lim + si * lre + vim_ref[row, :])
        return tuple(new)

    init = tuple(st_ref[k, pl.ds(b, 1), :] for b in range(batch) for k in range(2))
    fin = lax.fori_loop(0, n_chunks, step, init)
    for b in range(batch):
        st_ref[0, pl.ds(b, 1), :] = fin[2 * b]
        st_ref[1, pl.ds(b, 1), :] = fin[2 * b + 1]

    for g in range(G):
        y = jnp.dot(ug_ref[g], kt_ref[g], preferred_element_type=F32)
        y = y + jnp.dot(sre_ref[:, g * P:(g + 1) * P].astype(BF16), cre_ref[g], preferred_element_type=F32)
        y = y + jnp.dot(sim_ref[:, g * P:(g + 1) * P].astype(BF16), cim_ref[g], preferred_element_type=F32)
        yg_ref[g] = y

    def unpack(i, carry):
        b = i // (n_chunks // 8)
        n0 = (i % (n_chunks // 8)) * 8
        r0 = pl.multiple_of(b * n_chunks + n0, 8)
        for cb in range(G // gpl):
            for s in range(C):
                piece = jnp.concatenate(
                    [yg_ref[cb * gpl + gl, pl.ds(r0, 8), s * Hs:(s + 1) * Hs] for gl in range(gpl)], axis=1)
                y_ref[cb, b, pl.ds(s + C * n0, 8, stride=C), :] = piece
        return carry

    lax.fori_loop(0, batch * n_chunks // 8, unpack, 0)


def _s5_tables(a_re, a_im, b_re, b_im, c_re, c_im, log_dt):
    C = S5_CHUNK
    G, P = a_re.shape
    Hs = S5_GROUP
    dt = jnp.exp(log_dt)[:, None]
    zr, zi = a_re * dt, a_im * dt
    mag = jnp.exp(zr)
    lam_re, lam_im = mag * jnp.cos(zi), mag * jnp.sin(zi)
    nr, ni = lam_re - 1.0, lam_im
    den = a_re * a_re + a_im * a_im
    coef_re = (nr * a_re + ni * a_im) / den
    coef_im = (ni * a_re - nr * a_im) / den
    bb_re = coef_re[..., None] * b_re - coef_im[..., None] * b_im
    bb_im = coef_re[..., None] * b_im + coef_im[..., None] * b_re
    tau = jnp.arange(C + 1, dtype=F32)[:, None, None]
    pmag = jnp.exp(tau * zr[None])
    pw_re, pw_im = pmag * jnp.cos(tau * zi[None]), pmag * jnp.sin(tau * zi[None])
    lb_re = pw_re[..., None] * bb_re[None] - pw_im[..., None] * bb_im[None]
    lb_im = pw_re[..., None] * bb_im[None] + pw_im[..., None] * bb_re[None]
    kern = (jnp.einsum('ghp,tgpk->tghk', c_re, lb_re[:C]) - jnp.einsum('ghp,tgpk->tghk', c_im, lb_im[:C]))
    s = jnp.arange(C)
    lag = s[None, :] - s[:, None]
    kt = jnp.where((lag >= 0)[:, :, None, None, None], kern[jnp.clip(lag, 0, C - 1)], 0.0)
    kt = kt.transpose(2, 0, 4, 1, 3).reshape(G, C * Hs, C * Hs)
    bst_re = lb_re[:C][::-1].transpose(1, 0, 3, 2).reshape(G, C * Hs, P)
    bst_im = lb_im[:C][::-1].transpose(1, 0, 3, 2).reshape(G, C * Hs, P)
    cl_re = c_re[None] * pw_re[1:, :, None, :] - c_im[None] * pw_im[1:, :, None, :]
    cl_im = c_re[None] * pw_im[1:, :, None, :] + c_im[None] * pw_re[1:, :, None, :]
    cst_re = cl_re.transpose(1, 3, 0, 2).reshape(G, P, C * Hs)
    cst_im = (-cl_im).transpose(1, 3, 0, 2).reshape(G, P, C * Hs)
    return (kt.astype(BF16), bst_re.astype(BF16), bst_im.astype(BF16), cst_re.astype(BF16), cst_im.astype(BF16),
            pw_re[C], pw_im[C])


def s5_mixer(proj, B, L, a_re, a_im, b_re, b_im, c_re, c_im, log_dt, block=512):
    C, Hs, P = S5_CHUNK, S5_GROUP, S5_STATE
    G = a_re.shape[0]
    Wu = G * Hs
    block = min(block, L)
    nc = block // C
    R = B * nc
    W = C * Hs
    kt, bst_re, bst_im, cst_re, cst_im, l_re, l_im = _s5_tables(a_re, a_im, b_re, b_im, c_re, c_im, log_dt)
    ncb = Wu // LANES
    u_col = (proj.shape[1] - Wu) // LANES

    def table(*shape):
        return pl.BlockSpec(shape, lambda j: (0,) * len(shape), pipeline_mode=pl.Buffered(1))

    proj3 = proj.reshape(B, L, proj.shape[1])
    y = pl.pallas_call(
        functools.partial(_s5_kernel, G=G, batch=B, n_chunks=nc),
        grid=(L // block,),
        in_specs=[pl.BlockSpec((B, block, LANES), lambda j, cb=cb: (0, j, u_col + cb)) for cb in range(ncb)]
                 + [table(G, W, W), table(G, W, P), table(G, W, P), table(G, P, W), table(G, P, W),
                    table(1, G * P), table(1, G * P)],
        out_specs=pl.BlockSpec((ncb, B, block, LANES), lambda j: (0, 0, j, 0)),
        out_shape=jax.ShapeDtypeStruct((ncb, B, L, LANES), F32),
        scratch_shapes=[pltpu.VMEM((G, R, W), BF16),
                        pltpu.VMEM((R, G * P), F32), pltpu.VMEM((R, G * P), F32),
                        pltpu.VMEM((R, G * P), F32), pltpu.VMEM((R, G * P), F32),
                        pltpu.VMEM((G, R, W), F32),
                        pltpu.VMEM((2, B, G * P), F32)],
        compiler_params=_cparams("arbitrary"),
        name="s5_chunked",
    )(*([proj3] * ncb), kt, bst_re, bst_im, cst_re, cst_im,
      l_re.reshape(1, G * P), l_im.reshape(1, G * P))
    return y.reshape(ncb, B * L, LANES)


def _ab_out_kernel(ret_ref, y_ref, u_ref, d_ref, wglu_ref, bglu_ref, wo1_ref, wo2_ref, h_ref, o_ref):
    y = jnp.concatenate([y_ref[cb] for cb in range(y_ref.shape[0])], axis=1) + d_ref[...] * u_ref[...]
    g = jax.nn.gelu(y)
    z = jnp.dot(g.astype(BF16), wglu_ref[...], preferred_element_type=F32) + bglu_ref[...]
    s5 = g * jax.nn.sigmoid(z)
    acc = jnp.dot(ret_ref[...].astype(BF16), wo1_ref[...], preferred_element_type=F32)
    acc = acc + jnp.dot(s5.astype(BF16), wo2_ref[...], preferred_element_type=F32)
    o_ref[...] = h_ref[...] + acc


def ab_out(ret, y, proj, d_skip, w_glu, b_glu, w_out, h, tm=256):
    T, Wv = ret.shape
    ncb = y.shape[0]
    Ws = ncb * y.shape[2]
    D = h.shape[1]
    u_col = (proj.shape[1] - Ws) // Ws
    row = lambda w: pl.BlockSpec((tm, w), lambda i: (i, 0))
    full = lambda a, b: pl.BlockSpec((a, b), lambda i: (0, 0))
    return pl.pallas_call(
        _ab_out_kernel,
        grid=(T // tm,),
        in_specs=[row(Wv), pl.BlockSpec((ncb, tm, y.shape[2]), lambda i: (0, i, 0)),
                  pl.BlockSpec((tm, Ws), lambda i: (i, u_col)), full(1, Ws),
                  full(Ws, Ws), full(1, Ws), full(Wv, D), full(Ws, D), row(D)],
        out_specs=row(D),
        out_shape=jax.ShapeDtypeStruct((T, D), F32),
        compiler_params=_cparams("parallel"),
        name="ab_out",
    )(ret, y, proj, d_skip.reshape(1, Ws), w_glu.astype(BF16), b_glu.reshape(1, Ws),
      w_out[:Wv].astype(BF16), w_out[Wv:].astype(BF16), h)


def _attn_kernel(q_ref, k_ref, v_ref, bias_ref, o_ref, ob_ref, lse_ref, *, L, branches, interleave):
    scale = ATT_DH ** -0.5

    def block(bi, d, band, r, n):
        qstart = r + d * band * n
        kstart = r + d * band * jnp.maximum(n - 1, 0)
        q = (q_ref[pl.ds(qstart, band, stride=d), :] * scale).astype(BF16)
        kw = k_ref[pl.ds(kstart, 2 * band, stride=d), :].astype(BF16)
        vw = v_ref[pl.ds(kstart, 2 * band, stride=d), :].astype(BF16)
        s = lax.dot_general(q, kw, (((1,), (1,)), ((), ())), preferred_element_type=F32)
        s = s + bias_ref[jnp.minimum(n, 1)]
        m = jnp.max(s, axis=-1, keepdims=True)
        p = jnp.exp(s - m)
        l = jnp.sum(p, axis=-1, keepdims=True)
        o = jnp.dot(p.astype(BF16), vw, preferred_element_type=F32) * (1.0 / l)
        ob_ref[bi, pl.ds(qstart, band, stride=d), :] = o
        lse_ref[bi, pl.ds(qstart, band, stride=d), :] = m + jnp.log(l)

    for bi, (window, d) in enumerate(branches):
        band = window // d
        nb = (L // d) // band
        assert nb >= 2 and (d * nb) % interleave == 0

        def body(j, carry, bi=bi, d=d, band=band, nb=nb):
            for u in range(interleave):
                jj = j * interleave + u
                block(bi, d, band, jj // nb, jj % nb)
            return carry

        lax.fori_loop(0, (d * nb) // interleave, body, 0)

    tile = 512
    nbr = len(branches)

    def merge(i, carry):
        rows = pl.ds(pl.multiple_of(i * tile, tile), tile)
        lses = [lse_ref[b, rows, :] for b in range(nbr)]
        mx = functools.reduce(jnp.maximum, lses)
        ws = [jnp.exp(x - mx) for x in lses]
        inv = 1.0 / functools.reduce(lambda a, b: a + b, ws)
        acc = (ws[0] * inv) * ob_ref[0, rows, :]
        for b in range(1, nbr):
            acc = acc + (ws[b] * inv) * ob_ref[b, rows, :]
        o_ref[rows, :] = acc
        return carry

    lax.fori_loop(0, L // tile, merge, 0)


def dilated_attention(qkv, B, L, interleave=4):
    H, dh = ATT_HEADS, ATT_DH
    nbr = len(DILATED_BRANCHES)
    band = DILATED_BRANCHES[0][0] // DILATED_BRANCHES[0][1]
    assert all(w // d == band for w, d in DILATED_BRANCHES)
    qi = jnp.arange(band)[:, None]
    kj = jnp.arange(2 * band)[None, :]
    dist = band + qi - kj
    bias = jnp.stack([jnp.where(kj <= qi, 0.0, NEG_BIG),
                      jnp.where((dist >= 0) & (dist <= band), 0.0, NEG_BIG)]).astype(F32)

    def col(off):
        return pl.BlockSpec((L, dh), lambda b, h: (b, off + h))

    return pl.pallas_call(
        functools.partial(_attn_kernel, L=L, branches=DILATED_BRANCHES, interleave=interleave),
        grid=(B, H),
        in_specs=[col(0), col(H), col(2 * H),
                  pl.BlockSpec((2, band, 2 * band), lambda b, h: (0, 0, 0))],
        out_specs=pl.BlockSpec((L, dh), lambda b, h: (b, h)),
        out_shape=jax.ShapeDtypeStruct((B * L, H * dh), F32),
        scratch_shapes=[pltpu.VMEM((nbr, L, dh), F32), pltpu.VMEM((nbr, L, 1), F32)],
        compiler_params=_cparams("parallel", "parallel"),
        name="dilated_attention",
    )(qkv, qkv, qkv, bias)


def _router_kernel(h_ref, g_ref, whi_ref, wlo_ref, bias_ref, xc_ref):
    D = h_ref.shape[1]
    xn = _rms(h_ref[...], g_ref[...])
    xhi = xn.astype(BF16)
    xlo = (xn - xhi.astype(F32)).astype(BF16)
    xc_ref[:, :D] = xn
    whi = whi_ref[...]
    logits = (jnp.dot(xhi, whi, preferred_element_type=F32) + jnp.dot(xlo, whi, preferred_element_type=F32)
              + jnp.dot(xhi, wlo_ref[...], preferred_element_type=F32) + bias_ref[...])
    shape = logits.shape
    lane = lax.broadcasted_iota(jnp.int32, shape, 1)
    lanef = lane.astype(F32)
    far = float(LANES)

    def first_argmax(vals):
        mx = jnp.max(vals, axis=-1, keepdims=True)
        idx = jnp.min(jnp.where(vals == mx, lanef, far), axis=-1, keepdims=True)
        return mx, idx

    isg = lane < N_GROUPS
    gl = jnp.where(isg, logits, NEG_BIG)
    gmax, gsel = first_argmax(gl)
    gval = 1.0 / jnp.sum(jnp.where(isg, jnp.exp(gl - gmax), 0.0), axis=-1, keepdims=True)
    egroup = ((lane - N_GROUPS) >> 2).astype(F32)
    ise = (lane >= N_GROUPS) & (lane < N_GROUPS + N_EXPERTS) & (egroup == gsel)
    el = jnp.where(ise, logits, NEG_BIG)
    v1, i1 = first_argmax(el)
    el2 = jnp.where(lanef == i1, NEG_BIG, el)
    v2, i2 = first_argmax(el2)
    e21 = jnp.exp(v2 - v1)
    w1 = gval / (1.0 + e21)
    w2 = w1 * e21
    comb = jnp.where(lanef == i1, w1, 0.0) + jnp.where(lanef == i2, w2, 0.0)
    xc_ref[:, D:] = jnp.where(lane == 0, gsel, comb)


def moe_router(h, g, w_group, b_group, w_router, b_router, tm=512):
    T, D = h.shape
    wcat = jnp.concatenate([w_group, w_router.reshape(D, N_EXPERTS)], axis=1)
    wcat = jnp.pad(wcat, ((0, 0), (0, LANES - wcat.shape[1])))
    whi = wcat.astype(BF16)
    wlo = (wcat - whi.astype(F32)).astype(BF16)
    bias = jnp.pad(jnp.concatenate([b_group, b_router.reshape(N_EXPERTS)]), (0, LANES - N_GROUPS - N_EXPERTS))
    return pl.pallas_call(
        _router_kernel,
        grid=(T // tm,),
        in_specs=[pl.BlockSpec((tm, D), lambda i: (i, 0)),
                  pl.BlockSpec((1, D), lambda i: (0, 0)),
                  pl.BlockSpec((D, LANES), lambda i: (0, 0)),
                  pl.BlockSpec((D, LANES), lambda i: (0, 0)),
                  pl.BlockSpec((1, LANES), lambda i: (0, 0))],
        out_specs=pl.BlockSpec((tm, D + LANES), lambda i: (i, 0)),
        out_shape=jax.ShapeDtypeStruct((T, D + LANES), F32),
        compiler_params=_cparams("parallel"),
        name="moe_router",
    )(h, g.reshape(1, D), whi, wlo, bias.reshape(1, LANES))


def _plan_kernel(r_ref, plan_ref, carry_ref):
    @pl.when(pl.program_id(0) == 0)
    def _():
        carry_ref[...] = jnp.zeros_like(carry_ref)

    r = r_ref[...]
    tb = r.shape[0]
    lane = lax.broadcasted_iota(jnp.int32, r.shape, 1)
    gsel = jnp.sum(jnp.where(lane == 0, r, 0.0), axis=-1, keepdims=True)
    onehot = jnp.where(lane.astype(F32) == gsel, 1.0, 0.0)
    onehot = jnp.where(lane < N_GROUPS, onehot, 0.0)
    ri = lax.broadcasted_iota(jnp.int32, (tb, tb), 0)
    ci = lax.broadcasted_iota(jnp.int32, (tb, tb), 1)
    tri = jnp.where(ci < ri, 1.0, 0.0).astype(BF16)
    before = jnp.dot(tri, onehot.astype(BF16), preferred_element_type=F32) + carry_ref[...]
    rank = jnp.sum(onehot * before, axis=-1, keepdims=True)
    carry = carry_ref[...] + jnp.sum(onehot, axis=0, keepdims=True)
    carry_ref[...] = carry
    plan_ref[...] = jnp.where(lane == 0, gsel, 0.0) + jnp.where(lane == 1, rank, 0.0) \
        + jnp.where((lane >= N_GROUPS) & (lane < 2 * N_GROUPS), pltpu.roll(carry, N_GROUPS, axis=1), 0.0)


def moe_plan(xc, D, tb=512):
    T = xc.shape[0]
    tb = min(tb, T)
    return pl.pallas_call(
        _plan_kernel,
        grid=(T // tb,),
        in_specs=[pl.BlockSpec((tb, LANES), lambda i: (i, D // LANES))],
        out_specs=pl.BlockSpec((tb, LANES), lambda i: (i, 0)),
        out_shape=jax.ShapeDtypeStruct((T, LANES), F32),
        scratch_shapes=[pltpu.VMEM((1, LANES), F32)],
        compiler_params=_cparams("arbitrary"),
        name="moe_plan",
    )(xc)


MOE_DMA_CHUNK = 256
MOE_ROW_TILE = 512


def _row_copy_wait(src_hbm, dst_ref, sem, n):
    pltpu.make_async_copy(src_hbm.at[pl.ds(0, n)], dst_ref.at[pl.ds(0, n)], sem).wait()


def _dispatch_kernel(pos_ref, xc_hbm, xs_in, xs_out, sem, *, n_tokens):
    del xs_in
    ch = MOE_DMA_CHUNK
    n_chunks = n_tokens // ch

    def issue(c):
        def body(j, carry):
            t = c * ch + j
            pltpu.make_async_copy(xc_hbm.at[pl.ds(t, 1)], xs_out.at[pl.ds(pos_ref[t], 1)], sem.at[c % 2]).start()
            return carry
        lax.fori_loop(0, ch, body, 0, unroll=8)

    issue(0)

    def loop(c, carry):
        issue(c)
        _row_copy_wait(xc_hbm, xs_out, sem.at[(c - 1) % 2], ch)
        return carry

    lax.fori_loop(1, n_chunks, loop, 0)
    _row_copy_wait(xc_hbm, xs_out, sem.at[(n_chunks - 1) % 2], ch)


def moe_dispatch(xc, pos, n_rows):
    T, Wc = xc.shape
    assert T % MOE_DMA_CHUNK == 0
    return pl.pallas_call(
        functools.partial(_dispatch_kernel, n_tokens=T),
        grid_spec=pltpu.PrefetchScalarGridSpec(
            num_scalar_prefetch=1,
            grid=(1,),
            in_specs=[pl.BlockSpec(memory_space=pl.ANY), pl.BlockSpec(memory_space=pl.ANY)],
            out_specs=pl.BlockSpec(memory_space=pl.ANY),
            scratch_shapes=[pltpu.SemaphoreType.DMA((2,))]),
        out_shape=jax.ShapeDtypeStruct((n_rows, Wc), F32),
        input_output_aliases={2: 0},
        compiler_params=pltpu.CompilerParams(dimension_semantics=("arbitrary",), has_side_effects=True),
        name="moe_dispatch",
    )(pos, xc, jnp.zeros((n_rows, Wc), F32))


def _experts_kernel(tg_ref, nu_ref, xs_ref, wg_ref, wu_ref, wd_ref, ys_ref, xb_ref, acc_ref):
    i = pl.program_id(0)
    e = pl.program_id(1)
    D = xb_ref.shape[1]

    @pl.when(i < nu_ref[0])
    def _():
        @pl.when(e == 0)
        def _():
            xb_ref[...] = xs_ref[:, :D].astype(BF16)
            acc_ref[...] = jnp.zeros_like(acc_ref)

        comb = xs_ref[:, D:]
        lane = lax.broadcasted_iota(jnp.int32, comb.shape, 1)
        col = N_GROUPS + tg_ref[i] * EXPERTS_PER_GROUP + e
        c = jnp.sum(jnp.where(lane == col, comb, 0.0), axis=-1, keepdims=True)
        x = xb_ref[...]
        hg = jnp.dot(x, wg_ref[...], preferred_element_type=F32)
        hu = jnp.dot(x, wu_ref[...], preferred_element_type=F32)
        hh = hg * jax.nn.sigmoid(hg) * hu * c
        acc_ref[...] += jnp.dot(hh.astype(BF16), wd_ref[...], preferred_element_type=F32)

        @pl.when(e == pl.num_programs(1) - 1)
        def _():
            ys_ref[...] = acc_ref[...]

    @pl.when((i >= nu_ref[0]) & (e == pl.num_programs(1) - 1))
    def _():
        ys_ref[...] = jnp.zeros_like(ys_ref)


def moe_experts(xs, tile_group, n_used, w_gate, w_up, w_down, tm):
    P, Wc = xs.shape
    E, D, Fd = w_gate.shape
    nt = P // tm
    epg = EXPERTS_PER_GROUP

    def tile(i, e, tg, nu):
        return (jnp.minimum(i, nu[0] - 1), 0)

    def expert(i, e, tg, nu):
        last = nu[0] - 1
        return (jnp.where(i <= last, tg[jnp.minimum(i, last)] * epg + e, tg[last] * epg + epg - 1), 0, 0)

    return pl.pallas_call(
        _experts_kernel,
        grid_spec=pltpu.PrefetchScalarGridSpec(
            num_scalar_prefetch=2,
            grid=(nt, epg),
            in_specs=[pl.BlockSpec((tm, Wc), tile),
                      pl.BlockSpec((None, D, Fd), expert),
                      pl.BlockSpec((None, D, Fd), expert),
                      pl.BlockSpec((None, Fd, D), expert)],
            out_specs=pl.BlockSpec((tm, D), lambda i, e, tg, nu: (i, 0)),
            scratch_shapes=[pltpu.VMEM((tm, D), BF16), pltpu.VMEM((tm, D), F32)]),
        out_shape=jax.ShapeDtypeStruct((P, D), F32),
        compiler_params=_cparams("arbitrary", "arbitrary"),
        name="moe_experts",
    )(tile_group, n_used, xs, w_gate, w_up, w_down)


def _combine_kernel(pos_ref, ys_hbm, h_ref, fn_ref, o_ref, buf_ref, sem, *, final_norm):
    i = pl.program_id(0)
    tc = h_ref.shape[0]
    slot = i % 2

    def issue(tile, s):
        def body(j, carry):
            p = pos_ref[tile * tc + j]
            pltpu.make_async_copy(ys_hbm.at[pl.ds(p, 1)], buf_ref.at[s, pl.ds(j, 1)], sem.at[s]).start()
            return carry
        lax.fori_loop(0, tc, body, 0, unroll=8)

    @pl.when(i == 0)
    def _():
        issue(0, 0)

    @pl.when(i + 1 < pl.num_programs(0))
    def _():
        issue(i + 1, 1 - slot)

    _row_copy_wait(ys_hbm, buf_ref.at[slot], sem.at[slot], tc)
    out = h_ref[...] + buf_ref[slot]
    if final_norm:
        out = _rms(out, fn_ref[...])
    o_ref[...] = out


def moe_combine(ys, pos, h, fnorm, final_norm, tc=256):
    T, D = h.shape
    return pl.pallas_call(
        functools.partial(_combine_kernel, final_norm=final_norm),
        grid_spec=pltpu.PrefetchScalarGridSpec(
            num_scalar_prefetch=1,
            grid=(T // tc,),
            in_specs=[pl.BlockSpec(memory_space=pl.ANY),
                      pl.BlockSpec((tc, D), lambda i, pos: (i, 0)),
                      pl.BlockSpec((1, D), lambda i, pos: (0, 0))],
            out_specs=pl.BlockSpec((tc, D), lambda i, pos: (i, 0)),
            scratch_shapes=[pltpu.VMEM((2, tc, D), F32), pltpu.SemaphoreType.DMA((2,))]),
        out_shape=jax.ShapeDtypeStruct((T, D), F32),
        compiler_params=_cparams("arbitrary"),
        name="moe_combine",
    )(pos, ys, h, fnorm.reshape(1, D))


def hier_moe(h, g, w_group, b_group, w_router, b_router, w_gate, w_up, w_down, fnorm, final_norm):
    T, D = h.shape
    Fd = w_gate.shape[-1]
    tm = min(MOE_ROW_TILE, T)
    xc = moe_router(h, g, w_group, b_group, w_router, b_router)
    plan = moe_plan(xc, D)
    gsel = plan[:, 0].astype(jnp.int32)
    rank = plan[:, 1].astype(jnp.int32)
    counts = plan[T - 1, N_GROUPS:2 * N_GROUPS].astype(jnp.int32)
    tiles = (counts + tm - 1) // tm
    ends = jnp.cumsum(tiles)
    starts = ends - tiles
    pos = rank + tm * jnp.sum(jnp.where(gsel[:, None] == jnp.arange(N_GROUPS)[None, :], starts[None, :], 0), axis=1)
    nt = T // tm + N_GROUPS
    tile_group = jnp.minimum(jnp.sum(jnp.arange(nt)[:, None] >= ends[None, :], axis=1), N_GROUPS - 1)
    xs = moe_dispatch(xc, pos.astype(jnp.int32), nt * tm)
    ys = moe_experts(xs, tile_group.astype(jnp.int32), ends[-1:].astype(jnp.int32),
                     w_gate.reshape(N_EXPERTS, D, Fd).astype(BF16), w_up.reshape(N_EXPERTS, D, Fd).astype(BF16),
                     w_down.reshape(N_EXPERTS, Fd, D).astype(BF16), tm)
    return moe_combine(ys, pos.astype(jnp.int32), h, fnorm, final_norm)


def kernel(x, mix_norm, ffn_norm, final_norm, ab_w_in, ab_w_out, s5_a_re, s5_a_im, s5_b_re, s5_b_im,
           s5_c_re, s5_c_im, s5_d, s5_log_dt, s5_w_glu, s5_b_glu, c_w_qkv, c_w_out,
           moe_w_group, moe_b_group, moe_w_router, moe_b_router, moe_w_gate, moe_w_up, moe_w_down):
    B, L, D = x.shape
    h = x.reshape(B * L, D)
    depth = mix_norm.shape[0]
    for layer in range(depth):
        i = layer // 2
        if layer % 2 == 0:
            proj = norm_matmul(h, mix_norm[layer], ab_w_in[i].astype(BF16))
            ret = retention_mixer(proj, B, L)
            y = s5_mixer(proj, B, L, s5_a_re[i], s5_a_im[i], s5_b_re[i], s5_b_im[i],
                         s5_c_re[i], s5_c_im[i], s5_log_dt[i])
            h = ab_out(ret, y, proj, s5_d[i].reshape(-1), s5_w_glu[i], s5_b_glu[i], ab_w_out[i], h)
        else:
            qkv = norm_matmul(h, mix_norm[layer], c_w_qkv[i].astype(BF16))
            att = dilated_attention(qkv, B, L)
            h = matmul_residual(att, c_w_out[i].astype(BF16), h)
        h = hier_moe(h, ffn_norm[layer], moe_w_group[layer], moe_b_group[layer], moe_w_router[layer],
                     moe_b_router[layer], moe_w_gate[layer], moe_w_up[layer], moe_w_down[layer],
                     final_norm, layer == depth - 1)
    return h.reshape(B, L, D)
```

```python
import functools
import math

import jax
import jax.numpy as jnp
from jax import lax
from jax.experimental import pallas as pl
from jax.experimental.pallas import tpu as pltpu

F32 = jnp.float32
BF16 = jnp.bfloat16

RET_HEADS = 4
RET_DK = 128
ROPE_BASE = 10000.0
S5_GROUP = 16
S5_STATE = 64
ATT_HEADS = 8
ATT_DH = 128
DILATED_BRANCHES = ((128, 1), (512, 4), (2048, 16))
N_GROUPS = 4
EXPERTS_PER_GROUP = 4
N_EXPERTS = N_GROUPS * EXPERTS_PER_GROUP
RMS_EPS = 1e-6
GN_EPS = 1e-6

LANES = 128
VMEM_LIMIT = 48 * 1024 * 1024
S5_CHUNK = 16
RET_CHUNK = 256
NEG_BIG = -1e30


def _cparams(*sem):
    return pltpu.CompilerParams(dimension_semantics=sem, vmem_limit_bytes=VMEM_LIMIT)


def _rms(x, g):
    return x * lax.rsqrt(jnp.mean(x * x, axis=-1, keepdims=True) + RMS_EPS) * g


def _norm_matmul_kernel(x_ref, g_ref, w_ref, o_ref):
    y = _rms(x_ref[...], g_ref[...])
    o_ref[...] = jnp.dot(y.astype(BF16), w_ref[...], preferred_element_type=F32)


def norm_matmul(x, g, w_bf16, tm=256):
    T, D = x.shape
    N = w_bf16.shape[1]
    return pl.pallas_call(
        _norm_matmul_kernel,
        grid=(T // tm,),
        in_specs=[pl.BlockSpec((tm, D), lambda i: (i, 0)),
                  pl.BlockSpec((1, D), lambda i: (0, 0)),
                  pl.BlockSpec((D, N), lambda i: (0, 0))],
        out_specs=pl.BlockSpec((tm, N), lambda i: (i, 0)),
        out_shape=jax.ShapeDtypeStruct((T, N), F32),
        compiler_params=_cparams("parallel"),
        name="norm_matmul",
    )(x, g.reshape(1, D), w_bf16)


def _matmul_res_kernel(a_ref, w_ref, r_ref, o_ref):
    o_ref[...] = r_ref[...] + jnp.dot(a_ref[...].astype(BF16), w_ref[...], preferred_element_type=F32)


def matmul_residual(a, w_bf16, res, tm=256):
    T, K = a.shape
    N = w_bf16.shape[1]
    return pl.pallas_call(
        _matmul_res_kernel,
        grid=(T // tm,),
        in_specs=[pl.BlockSpec((tm, K), lambda i: (i, 0)),
                  pl.BlockSpec((K, N), lambda i: (0, 0)),
                  pl.BlockSpec((tm, N), lambda i: (i, 0))],
        out_specs=pl.BlockSpec((tm, N), lambda i: (i, 0)),
        out_shape=jax.ShapeDtypeStruct((T, N), F32),
        compiler_params=_cparams("parallel"),
        name="matmul_residual",
    )(a, w_bf16, res)


def _retention_kernel(q_ref, k_ref, v_ref, gate_ref, cos_ref, sin_ref, decay_ref, xi_ref, zeta_ref, cd_ref,
                      o_ref, state_ref, *, n_chunks, chunk):
    state_ref[...] = jnp.zeros_like(state_ref)
    decay = decay_ref[...]
    xi = xi_ref[...]
    zeta = zeta_ref[...]
    cd = cd_ref[...]
    half = RET_DK // 2

    def rot(x, cos, sin):
        return x * cos + pltpu.roll(x, half, axis=1) * sin

    def body(n, carry):
        rows = pl.ds(pl.multiple_of(n * chunk, chunk), chunk)
        cos = cos_ref[rows, :]
        sin = sin_ref[rows, :]
        q = rot(q_ref[rows, :], cos, sin)
        k = rot(k_ref[rows, :], cos, sin) * (RET_DK ** -0.5)
        v = v_ref[rows, :].astype(BF16)
        s = lax.dot_general(q.astype(BF16), k.astype(BF16), (((1,), (1,)), ((), ())),
                            preferred_element_type=F32) * decay
        state = state_ref[...]
        o = jnp.dot(s.astype(BF16), v, preferred_element_type=F32)
        o = o + jnp.dot((q * xi).astype(BF16), state.astype(BF16), preferred_element_type=F32)
        kv = lax.dot_general((k * zeta).astype(BF16), v, (((0,), (0,)), ((), ())),
                             preferred_element_type=F32)
        state_ref[...] = state * cd + kv
        mu = jnp.mean(o, axis=-1, keepdims=True)
        oc = o - mu
        var = jnp.mean(oc * oc, axis=-1, keepdims=True)
        on = oc * lax.rsqrt(var + GN_EPS)
        g = gate_ref[rows, :]
        o_ref[rows, :] = g * jax.nn.sigmoid(g) * on
        return carry

    lax.fori_loop(0, n_chunks, body, 0)


def retention_mixer(proj, B, L, chunk=RET_CHUNK):
    H, dk = RET_HEADS, RET_DK
    chunk = min(chunk, L)
    half = dk // 2
    pos = jnp.arange(L, dtype=F32)
    inv = ROPE_BASE ** (-jnp.arange(half, dtype=F32) / half)
    ang = pos[:, None] * inv[None, :]
    cos = jnp.concatenate([jnp.cos(ang), jnp.cos(ang)], axis=-1)
    sin = jnp.concatenate([-jnp.sin(ang), jnp.sin(ang)], axis=-1)
    lg = jnp.log1p(-jnp.exp2(-5.0 - jnp.arange(H, dtype=F32)))
    idx = jnp.arange(chunk, dtype=F32)
    diff = idx[:, None] - idx[None, :]
    decay = jnp.where(diff[None] >= 0, jnp.exp(jnp.maximum(diff, 0.0)[None] * lg[:, None, None]), 0.0)
    xi = jnp.exp((idx + 1.0)[None, :] * lg[:, None])[..., None]
    zeta = jnp.exp((chunk - 1.0 - idx)[None, :] * lg[:, None])[..., None]
    cd = jnp.exp(chunk * lg)[:, None, None]

    def col(off):
        return pl.BlockSpec((L, dk), lambda b, h: (b, off + h))

    tab = pl.BlockSpec((L, dk), lambda b, h: (0, 0))
    return pl.pallas_call(
        functools.partial(_retention_kernel, n_chunks=L // chunk, chunk=chunk),
        grid=(B, H),
        in_specs=[col(0), col(H), col(2 * H), col(3 * H), tab, tab,
                  pl.BlockSpec((None, chunk, chunk), lambda b, h: (h, 0, 0)),
                  pl.BlockSpec((None, chunk, 1), lambda b, h: (h, 0, 0)),
                  pl.BlockSpec((None, chunk, 1), lambda b, h: (h, 0, 0)),
                  pl.BlockSpec((None, 1, 1), lambda b, h: (h, 0, 0))],
        out_specs=pl.BlockSpec((L, dk), lambda b, h: (b, h)),
        out_shape=jax.ShapeDtypeStruct((B * L, H * dk), F32),
        scratch_shapes=[pltpu.VMEM((dk, dk), F32)],
        compiler_params=_cparams("parallel", "parallel"),
        name="retention",
    )(proj, proj, proj, proj, cos, sin, decay, xi, zeta, cd)


def _s5_kernel(*refs, G, batch, n_chunks):
    ncb = G * S5_GROUP // LANES
    u_refs = refs[:ncb]
    (kt_ref, bre_ref, bim_ref, cre_ref, cim_ref, lre_ref, lim_ref, y_ref,
     ug_ref, vre_ref, vim_ref, sre_ref, sim_ref, yg_ref, st_ref) = refs[ncb:]
    C, Hs, P = S5_CHUNK, S5_GROUP, S5_STATE
    W = C * Hs
    gpl = LANES // Hs
    spl = LANES // Hs
    rb = 16

    @pl.when(pl.program_id(0) == 0)
    def _():
        st_ref[...] = jnp.zeros_like(st_ref)

    def pack(i, carry):
        b = i // (n_chunks // rb)
        n0 = (i % (n_chunks // rb)) * rb
        r0 = pl.multiple_of(b * n_chunks + n0, rb)
        for cb in range(G // gpl):
            xs = [u_refs[cb][b, pl.ds(s + C * n0, rb, stride=C), :] for s in range(C)]
            for gl in range(gpl):
                for half in range(C // spl):
                    piece = jnp.concatenate(
                        [xs[half * spl + k][:, gl * Hs:(gl + 1) * Hs] for k in range(spl)], axis=1)
                    ug_ref[cb * gpl + gl, pl.ds(r0, rb), half * LANES:(half + 1) * LANES] = piece.astype(BF16)
        return carry

    lax.fori_loop(0, batch * n_chunks // rb, pack, 0)

    for g in range(G):
        u = ug_ref[g]
        vre_ref[:, g * P:(g + 1) * P] = jnp.dot(u, bre_ref[g], preferred_element_type=F32)
        vim_ref[:, g * P:(g + 1) * P] = jnp.dot(u, bim_ref[g], preferred_element_type=F32)

    lre = lre_ref[...]
    lim = lim_ref[...]

    def step(n, carry):
        new = []
        for b in range(batch):
            sr, si = carry[2 * b], carry[2 * b + 1]
            row = pl.ds(b * n_chunks + n, 1)
            sre_ref[row, :] = sr
            sim_ref[row, :] = si
            new.append(sr * lre - si * lim + vre_ref[row, :])
            new.append(sr * lim + si * lre + vim_ref[row, :])
        return tuple(new)

    init = tuple(st_ref[k, pl.ds(b, 1), :] for b in range(batch) for k in range(2))
    fin = lax.fori_loop(0, n_chunks, step, init)
    for b in range(batch):
        st_ref[0, pl.ds(b, 1), :] = fin[2 * b]
        st_ref[1, pl.ds(b, 1), :] = fin[2 * b + 1]

    for g in range(G):
        y = jnp.dot(ug_ref[g], kt_ref[g], preferred_element_type=F32)
        y = y + jnp.dot(sre_ref[:, g * P:(g + 1) * P].astype(BF16), cre_ref[g], preferred_element_type=F32)
        y = y + jnp.dot(sim_ref[:, g * P:(g + 1) * P].astype(BF16), cim_ref[g], preferred_element_type=F32)
        yg_ref[g] = y

    def unpack(i, carry):
        b = i // (n_chunks // 8)
        n0 = (i % (n_chunks // 8)) * 8
        r0 = pl.multiple_of(b * n_chunks + n0, 8)
        for cb in range(G // gpl):
            for s in range(C):
                piece = jnp.concatenate(
                    [yg_ref[cb * gpl + gl, pl.ds(r0, 8), s * Hs:(s + 1) * Hs] for gl in range(gpl)], axis=1)
                y_ref[cb, b, pl.ds(s + C * n0, 8, stride=C), :] = piece
        return carry

    lax.fori_loop(0, batch * n_chunks // 8, unpack, 0)


def _s5_tables(a_re, a_im, b_re, b_im, c_re, c_im, log_dt):
    C = S5_CHUNK
    G, P = a_re.shape
    Hs = S5_GROUP
    dt = jnp.exp(log_dt)[:, None]
    zr, zi = a_re * dt, a_im * dt
    mag = jnp.exp(zr)
    lam_re, lam_im = mag * jnp.cos(zi), mag * jnp.sin(zi)
    nr, ni = lam_re - 1.0, lam_im
    den = a_re * a_re + a_im * a_im
    coef_re = (nr * a_re + ni * a_im) / den
    coef_im = (ni * a_re - nr * a_im) / den
    bb_re = coef_re[..., None] * b_re - coef_im[..., None] * b_im
    bb_im = coef_re[..., None] * b_im + coef_im[..., None] * b_re
    tau = jnp.arange(C + 1, dtype=F32)[:, None, None]
    pmag = jnp.exp(tau * zr[None])
    pw_re, pw_im = pmag * jnp.cos(tau * zi[None]), pmag * jnp.sin(tau * zi[None])
    lb_re = pw_re[..., None] * bb_re[None] - pw_im[..., None] * bb_im[None]
    lb_im = pw_re[..., None] * bb_im[None] + pw_im[..., None] * bb_re[None]
    kern = (jnp.einsum('ghp,tgpk->tghk', c_re, lb_re[:C]) - jnp.einsum('ghp,tgpk->tghk', c_im, lb_im[:C]))
    s = jnp.arange(C)
    lag = s[None, :] - s[:, None]
    kt = jnp.where((lag >= 0)[:, :, None, None, None], kern[jnp.clip(lag, 0, C - 1)], 0.0)
    kt = kt.transpose(2, 0, 4, 1, 3).reshape(G, C * Hs, C * Hs)
    bst_re = lb_re[:C][::-1].transpose(1, 0, 3, 2).reshape(G, C * Hs, P)
    bst_im = lb_im[:C][::-1].transpose(1, 0, 3, 2).reshape(G, C * Hs, P)
    cl_re = c_re[None] * pw_re[1:, :, None, :] - c_im[None] * pw_im[1:, :, None, :]
    cl_im = c_re[None] * pw_im[1:, :, None, :] + c_im[None] * pw_re[1:, :, None, :]
    cst_re = cl_re.transpose(1, 3, 0, 2).reshape(G, P, C * Hs)
    cst_im = (-cl_im).transpose(1, 3, 0, 2).reshape(G, P, C * Hs)
    return (kt.astype(BF16), bst_re.astype(BF16), bst_im.astype(BF16), cst_re.astype(BF16), cst_im.astype(BF16),
            pw_re[C], pw_im[C])


def s5_mixer(proj, B, L, a_re, a_im, b_re, b_im, c_re, c_im, log_dt, block=512):
    C, Hs, P = S5_CHUNK, S5_GROUP, S5_STATE
    G = a_re.shape[0]
    Wu = G * Hs
    block = min(block, L)
    nc = block // C
    R = B * nc
    W = C * Hs
    kt, bst_re, bst_im, cst_re, cst_im, l_re, l_im = _s5_tables(a_re, a_im, b_re, b_im, c_re, c_im, log_dt)
    ncb = Wu // LANES
    u_col = (proj.shape[1] - Wu) // LANES

    def table(*shape):
        return pl.BlockSpec(shape, lambda j: (0,) * len(shape), pipeline_mode=pl.Buffered(1))

    proj3 = proj.reshape(B, L, proj.shape[1])
    y = pl.pallas_call(
        functools.partial(_s5_kernel, G=G, batch=B, n_chunks=nc),
        grid=(L // block,),
        in_specs=[pl.BlockSpec((B, block, LANES), lambda j, cb=cb: (0, j, u_col + cb)) for cb in range(ncb)]
                 + [table(G, W, W), table(G, W, P), table(G, W, P), table(G, P, W), table(G, P, W),
                    table(1, G * P), table(1, G * P)],
        out_specs=pl.BlockSpec((ncb, B, block, LANES), lambda j: (0, 0, j, 0)),
        out_shape=jax.ShapeDtypeStruct((ncb, B, L, LANES), F32),
        scratch_shapes=[pltpu.VMEM((G, R, W), BF16),
                        pltpu.VMEM((R, G * P), F32), pltpu.VMEM((R, G * P), F32),
                        pltpu.VMEM((R, G * P), F32), pltpu.VMEM((R, G * P), F32),
                        pltpu.VMEM((G, R, W), F32),
                        pltpu.VMEM((2, B, G * P), F32)],
        compiler_params=_cparams("arbitrary"),
        name="s5_chunked",
    )(*([proj3] * ncb), kt, bst_re, bst_im, cst_re, cst_im,
      l_re.reshape(1, G * P), l_im.reshape(1, G * P))
    return y.reshape(ncb, B * L, LANES)


def _ab_out_kernel(ret_ref, y_ref, u_ref, d_ref, wglu_ref, bglu_ref, wo1_ref, wo2_ref, h_ref, o_ref):
    y = jnp.concatenate([y_ref[cb] for cb in range(y_ref.shape[0])], axis=1) + d_ref[...] * u_ref[...]
    g = jax.nn.gelu(y)
    z = jnp.dot(g.astype(BF16), wglu_ref[...], preferred_element_type=F32) + bglu_ref[...]
    s5 = g * jax.nn.sigmoid(z)
    acc = jnp.dot(ret_ref[...].astype(BF16), wo1_ref[...], preferred_element_type=F32)
    acc = acc + jnp.dot(s5.astype(BF16), wo2_ref[...], preferred_element_type=F32)
    o_ref[...] = h_ref[...] + acc


def ab_out(ret, y, proj, d_skip, w_glu, b_glu, w_out, h, tm=256):
    T, Wv = ret.shape
    ncb = y.shape[0]
    Ws = ncb * y.shape[2]
    D = h.shape[1]
    u_col = (proj.shape[1] - Ws) // Ws
    row = lambda w: pl.BlockSpec((tm, w), lambda i: (i, 0))
    full = lambda a, b: pl.BlockSpec((a, b), lambda i: (0, 0))
    return pl.pallas_call(
        _ab_out_kernel,
        grid=(T // tm,),
        in_specs=[row(Wv), pl.BlockSpec((ncb, tm, y.shape[2]), lambda i: (0, i, 0)),
                  pl.BlockSpec((tm, Ws), lambda i: (i, u_col)), full(1, Ws),
                  full(Ws, Ws), full(1, Ws), full(Wv, D), full(Ws, D), row(D)],
        out_specs=row(D),
        out_shape=jax.ShapeDtypeStruct((T, D), F32),
        compiler_params=_cparams("parallel"),
        name="ab_out",
    )(ret, y, proj, d_skip.reshape(1, Ws), w_glu.astype(BF16), b_glu.reshape(1, Ws),
      w_out[:Wv].astype(BF16), w_out[Wv:].astype(BF16), h)


def _attn_kernel(q_ref, k_ref, v_ref, bias_ref, o_ref, ob_ref, lse_ref, *, L, branches, interleave):
    scale = ATT_DH ** -0.5

    def block(bi, d, band, r, n):
        qstart = r + d * band * n
        kstart = r + d * band * jnp.maximum(n - 1, 0)
        q = (q_ref[pl.ds(qstart, band, stride=d), :] * scale).astype(BF16)
        kw = k_ref[pl.ds(kstart, 2 * band, stride=d), :].astype(BF16)
        vw = v_ref[pl.ds(kstart, 2 * band, stride=d), :].astype(BF16)
        s = lax.dot_general(q, kw, (((1,), (1,)), ((), ())), preferred_element_type=F32)
        s = s + bias_ref[jnp.minimum(n, 1)]
        m = jnp.max(s, axis=-1, keepdims=True)
        p = jnp.exp(s - m)
        l = jnp.sum(p, axis=-1, keepdims=True)
        o = jnp.dot(p.astype(BF16), vw, preferred_element_type=F32) * (1.0 / l)
        ob_ref[bi, pl.ds(qstart, band, stride=d), :] = o
        lse_ref[bi, pl.ds(qstart, band, stride=d), :] = m + jnp.log(l)

    for bi, (window, d) in enumerate(branches):
        band = window // d
        nb = (L // d) // band
        assert nb >= 2 and (d * nb) % interleave == 0

        def body(j, carry, bi=bi, d=d, band=band, nb=nb):
            for u in range(interleave):
                jj = j * interleave + u
                block(bi, d, band, jj // nb, jj % nb)
            return carry

        lax.fori_loop(0, (d * nb) // interleave, body, 0)

    tile = 512
    nbr = len(branches)

    def merge(i, carry):
        rows = pl.ds(pl.multiple_of(i * tile, tile), tile)
        lses = [lse_ref[b, rows, :] for b in range(nbr)]
        mx = functools.reduce(jnp.maximum, lses)
        ws = [jnp.exp(x - mx) for x in lses]
        inv = 1.0 / functools.reduce(lambda a, b: a + b, ws)
        acc = (ws[0] * inv) * ob_ref[0, rows, :]
        for b in range(1, nbr):
            acc = acc + (ws[b] * inv) * ob_ref[b, rows, :]
        o_ref[rows, :] = acc
        return carry

    lax.fori_loop(0, L // tile, merge, 0)


def dilated_attention(qkv, B, L, interleave=4):
    H, dh = ATT_HEADS, ATT_DH
    nbr = len(DILATED_BRANCHES)
    band = DILATED_BRANCHES[0][0] // DILATED_BRANCHES[0][1]
    assert all(w // d == band for w, d in DILATED_BRANCHES)
    qi = jnp.arange(band)[:, None]
    kj = jnp.arange(2 * band)[None, :]
    dist = band + qi - kj
    bias = jnp.stack([jnp.where(kj <= qi, 0.0, NEG_BIG),
                      jnp.where((dist >= 0) & (dist <= band), 0.0, NEG_BIG)]).astype(F32)

    def col(off):
        return pl.BlockSpec((L, dh), lambda b, h: (b, off + h))

    return pl.pallas_call(
        functools.partial(_attn_kernel, L=L, branches=DILATED_BRANCHES, interleave=interleave),
        grid=(B, H),
        in_specs=[col(0), col(H), col(2 * H),
                  pl.BlockSpec((2, band, 2 * band), lambda b, h: (0, 0, 0))],
        out_specs=pl.BlockSpec((L, dh), lambda b, h: (b, h)),
        out_shape=jax.ShapeDtypeStruct((B * L, H * dh), F32),
        scratch_shapes=[pltpu.VMEM((nbr, L, dh), F32), pltpu.VMEM((nbr, L, 1), F32)],
        compiler_params=_cparams("parallel", "parallel"),
        name="dilated_attention",
    )(qkv, qkv, qkv, bias)


def _router_kernel(h_ref, g_ref, whi_ref, wlo_ref, bias_ref, xc_ref):
    D = h_ref.shape[1]
    xn = _rms(h_ref[...], g_ref[...])
    xhi = xn.astype(BF16)
    xlo = (xn - xhi.astype(F32)).astype(BF16)
    xc_ref[:, :D] = xn
    whi = whi_ref[...]
    logits = (jnp.dot(xhi, whi, preferred_element_type=F32) + jnp.dot(xlo, whi, preferred_element_type=F32)
              + jnp.dot(xhi, wlo_ref[...], preferred_element_type=F32) + bias_ref[...])
    shape = logits.shape
    lane = lax.broadcasted_iota(jnp.int32, shape, 1)
    lanef = lane.astype(F32)
    far = float(LANES)

    def first_argmax(vals):
        mx = jnp.max(vals, axis=-1, keepdims=True)
        idx = jnp.min(jnp.where(vals == mx, lanef, far), axis=-1, keepdims=True)
        return mx, idx

    isg = lane < N_GROUPS
    gl = jnp.where(isg, logits, NEG_BIG)
    gmax, gsel = first_argmax(gl)
    gval = 1.0 / jnp.sum(jnp.where(isg, jnp.exp(gl - gmax), 0.0), axis=-1, keepdims=True)
    egroup = ((lane - N_GROUPS) >> 2).astype(F32)
    ise = (lane >= N_GROUPS) & (lane < N_GROUPS + N_EXPERTS) & (egroup == gsel)
    el = jnp.where(ise, logits, NEG_BIG)
    v1, i1 = first_argmax(el)
    el2 = jnp.where(lanef == i1, NEG_BIG, el)
    v2, i2 = first_argmax(el2)
    e21 = jnp.exp(v2 - v1)
    w1 = gval / (1.0 + e21)
    w2 = w1 * e21
    comb = jnp.where(lanef == i1, w1, 0.0) + jnp.where(lanef == i2, w2, 0.0)
    xc_ref[:, D:] = jnp.where(lane == 0, gsel, comb)


def moe_router(h, g, w_group, b_group, w_router, b_router, tm=512):
    T, D = h.shape
    wcat = jnp.concatenate([w_group, w_router.reshape(D, N_EXPERTS)], axis=1)
    wcat = jnp.pad(wcat, ((0, 0), (0, LANES - wcat.shape[1])))
    whi = wcat.astype(BF16)
    wlo = (wcat - whi.astype(F32)).astype(BF16)
    bias = jnp.pad(jnp.concatenate([b_group, b_router.reshape(N_EXPERTS)]), (0, LANES - N_GROUPS - N_EXPERTS))
    return pl.pallas_call(
        _router_kernel,
        grid=(T // tm,),
        in_specs=[pl.BlockSpec((tm, D), lambda i: (i, 0)),
                  pl.BlockSpec((1, D), lambda i: (0, 0)),
                  pl.BlockSpec((D, LANES), lambda i: (0, 0)),
                  pl.BlockSpec((D, LANES), lambda i: (0, 0)),
                  pl.BlockSpec((1, LANES), lambda i: (0, 0))],
        out_specs=pl.BlockSpec((tm, D + LANES), lambda i: (i, 0)),
        out_shape=jax.ShapeDtypeStruct((T, D + LANES), F32),
        compiler_params=_cparams("parallel"),
        name="moe_router",
    )(h, g.reshape(1, D), whi, wlo, bias.reshape(1, LANES))


def _plan_kernel(r_ref, plan_ref, carry_ref):
    @pl.when(pl.program_id(0) == 0)
    def _():
        carry_ref[...] = jnp.zeros_like(carry_ref)

    r = r_ref[...]
    tb = r.shape[0]
    lane = lax.broadcasted_iota(jnp.int32, r.shape, 1)
    gsel = jnp.sum(jnp.where(lane == 0, r, 0.0), axis=-1, keepdims=True)
    onehot = jnp.where(lane.astype(F32) == gsel, 1.0, 0.0)
    onehot = jnp.where(lane < N_GROUPS, onehot, 0.0)
    ri = lax.broadcasted_iota(jnp.int32, (tb, tb), 0)
    ci = lax.broadcasted_iota(jnp.int32, (tb, tb), 1)
    tri = jnp.where(ci < ri, 1.0, 0.0).astype(BF16)
    before = jnp.dot(tri, onehot.astype(BF16), preferred_element_type=F32) + carry_ref[...]
    rank = jnp.sum(onehot * before, axis=-1, keepdims=True)
    carry = carry_ref[...] + jnp.sum(onehot, axis=0, keepdims=True)
    carry_ref[...] = carry
    plan_ref[...] = jnp.where(lane == 0, gsel, 0.0) + jnp.where(lane == 1, rank, 0.0) \
        + jnp.where((lane >= N_GROUPS) & (lane < 2 * N_GROUPS), pltpu.roll(carry, N_GROUPS, axis=1), 0.0)


def moe_plan(xc, D, tb=512):
    T = xc.shape[0]
    tb = min(tb, T)
    return pl.pallas_call(
        _plan_kernel,
        grid=(T // tb,),
        in_specs=[pl.BlockSpec((tb, LANES), lambda i: (i, D // LANES))],
        out_specs=pl.BlockSpec((tb, LANES), lambda i: (i, 0)),
        out_shape=jax.ShapeDtypeStruct((T, LANES), F32),
        scratch_shapes=[pltpu.VMEM((1, LANES), F32)],
        compiler_params=_cparams("arbitrary"),
        name="moe_plan",
    )(xc)


MOE_ROW_TILE = 512


def _row_copy_wait(src_hbm, dst_ref, sem, n):
    pltpu.make_async_copy(src_hbm.at[pl.ds(0, n)], dst_ref.at[pl.ds(0, n)], sem).wait()


def _perm_kernel(pos_ref, perm_ref, *, n_tokens, n_rows):
    def clear(p, carry):
        perm_ref[p] = 0
        return carry

    lax.fori_loop(0, n_rows, clear, 0, unroll=8)

    def put(t, carry):
        perm_ref[pos_ref[t]] = t
        return carry

    lax.fori_loop(0, n_tokens, put, 0, unroll=8)


def moe_perm(pos, n_rows):
    T = pos.shape[0]
    return pl.pallas_call(
        functools.partial(_perm_kernel, n_tokens=T, n_rows=n_rows),
        grid_spec=pltpu.PrefetchScalarGridSpec(
            num_scalar_prefetch=1,
            grid=(1,),
            in_specs=[],
            out_specs=pl.BlockSpec(memory_space=pltpu.SMEM)),
        out_shape=jax.ShapeDtypeStruct((n_rows,), jnp.int32),
        compiler_params=pltpu.CompilerParams(dimension_semantics=("arbitrary",)),
        name="moe_perm",
    )(pos)


def _experts_kernel(tg_ref, nu_ref, perm_ref, xc_hbm, wg_ref, wu_ref, wd_ref, ys_ref,
                    buf_ref, xb_ref, acc_ref, sem):
    i = pl.program_id(0)
    e = pl.program_id(1)
    tm, D = xb_ref.shape
    slot = i % 2

    def gather(tile, s):
        def body(j, carry):
            src = perm_ref[tile * tm + j]
            pltpu.make_async_copy(xc_hbm.at[pl.ds(src, 1)], buf_ref.at[s, pl.ds(j, 1)], sem.at[s]).start()
            return carry
        lax.fori_loop(0, tm, body, 0, unroll=8)

    @pl.when((i == 0) & (e == 0))
    def _():
        gather(0, 0)

    @pl.when((e == 0) & (i + 1 < nu_ref[0]))
    def _():
        gather(i + 1, 1 - slot)

    @pl.when(i < nu_ref[0])
    def _():
        @pl.when(e == 0)
        def _():
            _row_copy_wait(xc_hbm, buf_ref.at[slot], sem.at[slot], tm)
            xb_ref[...] = buf_ref[slot, :, :D].astype(BF16)
            acc_ref[...] = jnp.zeros_like(acc_ref)

        comb = buf_ref[slot, :, D:]
        lane = lax.broadcasted_iota(jnp.int32, comb.shape, 1)
        col = N_GROUPS + tg_ref[i] * EXPERTS_PER_GROUP + e
        c = jnp.sum(jnp.where(lane == col, comb, 0.0), axis=-1, keepdims=True)
        x = xb_ref[...]
        hg = jnp.dot(x, wg_ref[...], preferred_element_type=F32)
        hu = jnp.dot(x, wu_ref[...], preferred_element_type=F32)
        hh = hg * jax.nn.sigmoid(hg) * hu * c
        acc_ref[...] += jnp.dot(hh.astype(BF16), wd_ref[...], preferred_element_type=F32)

        @pl.when(e == pl.num_programs(1) - 1)
        def _():
            ys_ref[...] = acc_ref[...]

    @pl.when((i >= nu_ref[0]) & (e == pl.num_programs(1) - 1))
    def _():
        ys_ref[...] = jnp.zeros_like(ys_ref)


def moe_experts(xc, perm, tile_group, n_used, w_gate, w_up, w_down, tm):
    P = perm.shape[0]
    Wc = xc.shape[1]
    E, D, Fd = w_gate.shape
    nt = P // tm
    epg = EXPERTS_PER_GROUP

    def expert(i, e, tg, nu, perm):
        last = nu[0] - 1
        return (jnp.where(i <= last, tg[jnp.minimum(i, last)] * epg + e, tg[last] * epg + epg - 1), 0, 0)

    return pl.pallas_call(
        _experts_kernel,
        grid_spec=pltpu.PrefetchScalarGridSpec(
            num_scalar_prefetch=3,
            grid=(nt, epg),
            in_specs=[pl.BlockSpec(memory_space=pl.ANY),
                      pl.BlockSpec((None, D, Fd), expert),
                      pl.BlockSpec((None, D, Fd), expert),
                      pl.BlockSpec((None, Fd, D), expert)],
            out_specs=pl.BlockSpec((tm, D), lambda i, e, tg, nu, perm: (i, 0)),
            scratch_shapes=[pltpu.VMEM((2, tm, Wc), F32), pltpu.VMEM((tm, D), BF16), pltpu.VMEM((tm, D), F32),
                            pltpu.SemaphoreType.DMA((2,))]),
        out_shape=jax.ShapeDtypeStruct((P, D), F32),
        compiler_params=_cparams("arbitrary", "arbitrary"),
        name="moe_experts",
    )(tile_group, n_used, perm, xc, w_gate, w_up, w_down)


def _combine_kernel(pos_ref, ys_hbm, h_ref, fn_ref, o_ref, buf_ref, sem, *, final_norm):
    i = pl.program_id(0)
    tc = h_ref.shape[0]
    slot = i % 2

    def issue(tile, s):
        def body(j, carry):
            p = pos_ref[tile * tc + j]
            pltpu.make_async_copy(ys_hbm.at[pl.ds(p, 1)], buf_ref.at[s, pl.ds(j, 1)], sem.at[s]).start()
            return carry
        lax.fori_loop(0, tc, body, 0, unroll=8)

    @pl.when(i == 0)
    def _():
        issue(0, 0)

    @pl.when(i + 1 < pl.num_programs(0))
    def _():
        issue(i + 1, 1 - slot)

    _row_copy_wait(ys_hbm, buf_ref.at[slot], sem.at[slot], tc)
    out = h_ref[...] + buf_ref[slot]
    if final_norm:
        out = _rms(out, fn_ref[...])
    o_ref[...] = out


def moe_combine(ys, pos, h, fnorm, final_norm, tc=256):
    T, D = h.shape
    return pl.pallas_call(
        functools.partial(_combine_kernel, final_norm=final_norm),
        grid_spec=pltpu.PrefetchScalarGridSpec(
            num_scalar_prefetch=1,
            grid=(T // tc,),
            in_specs=[pl.BlockSpec(memory_space=pl.ANY),
                      pl.BlockSpec((tc, D), lambda i, pos: (i, 0)),
                      pl.BlockSpec((1, D), lambda i, pos: (0, 0))],
            out_specs=pl.BlockSpec((tc, D), lambda i, pos: (i, 0)),
            scratch_shapes=[pltpu.VMEM((2, tc, D), F32), pltpu.SemaphoreType.DMA((2,))]),
        out_shape=jax.ShapeDtypeStruct((T, D), F32),
        compiler_params=_cparams("arbitrary"),
        name="moe_combine",
    )(pos, ys, h, fnorm.reshape(1, D))


def hier_moe(h, g, w_group, b_group, w_router, b_router, w_gate, w_up, w_down, fnorm, final_norm):
    T, D = h.shape
    Fd = w_gate.shape[-1]
    tm = min(MOE_ROW_TILE, T)
    xc = moe_router(h, g, w_group, b_group, w_router, b_router)
    plan = moe_plan(xc, D)
    gsel = plan[:, 0].astype(jnp.int32)
    rank = plan[:, 1].astype(jnp.int32)
    counts = plan[T - 1, N_GROUPS:2 * N_GROUPS].astype(jnp.int32)
    tiles = (counts + tm - 1) // tm
    ends = jnp.cumsum(tiles)
    starts = ends - tiles
    pos = rank + tm * jnp.sum(jnp.where(gsel[:, None] == jnp.arange(N_GROUPS)[None, :], starts[None, :], 0), axis=1)
    nt = T // tm + N_GROUPS
    tile_group = jnp.minimum(jnp.sum(jnp.arange(nt)[:, None] >= ends[None, :], axis=1), N_GROUPS - 1)
    perm = moe_perm(pos.astype(jnp.int32), nt * tm)
    ys = moe_experts(xc, perm, tile_group.astype(jnp.int32), ends[-1:].astype(jnp.int32),
                     w_gate.reshape(N_EXPERTS, D, Fd).astype(BF16), w_up.reshape(N_EXPERTS, D, Fd).astype(BF16),
                     w_down.reshape(N_EXPERTS, Fd, D).astype(BF16), tm)
    return moe_combine(ys, pos.astype(jnp.int32), h, fnorm, final_norm)


def kernel(x, mix_norm, ffn_norm, final_norm, ab_w_in, ab_w_out, s5_a_re, s5_a_im, s5_b_re, s5_b_im,
           s5_c_re, s5_c_im, s5_d, s5_log_dt, s5_w_glu, s5_b_glu, c_w_qkv, c_w_out,
           moe_w_group, moe_b_group, moe_w_router, moe_b_router, moe_w_gate, moe_w_up, moe_w_down):
    B, L, D = x.shape
    h = x.reshape(B * L, D)
    depth = mix_norm.shape[0]
    for layer in range(depth):
        i = layer // 2
        if layer % 2 == 0:
            proj = norm_matmul(h, mix_norm[layer], ab_w_in[i].astype(BF16))
            ret = retention_mixer(proj, B, L)
            y = s5_mixer(proj, B, L, s5_a_re[i], s5_a_im[i], s5_b_re[i], s5_b_im[i],
                         s5_c_re[i], s5_c_im[i], s5_log_dt[i])
            h = ab_out(ret, y, proj, s5_d[i].reshape(-1), s5_w_glu[i], s5_b_glu[i], ab_w_out[i], h)
        else:
            qkv = norm_matmul(h, mix_norm[layer], c_w_qkv[i].astype(BF16))
            att = dilated_attention(qkv, B, L)
            h = matmul_residual(att, c_w_out[i].astype(BF16), h)
        h = hier_moe(h, ffn_norm[layer], moe_w_group[layer], moe_b_group[layer], moe_w_router[layer],
                     moe_b_router[layer], moe_w_gate[layer], moe_w_up[layer], moe_w_down[layer],
                     final_norm, layer == depth - 1)
    return h.reshape(B, L, D)
```
